```python
import math
import jax, jax.numpy as jnp
from jax import lax
import numpy as np

D_MODEL = 1024
BATCH = 4
SEQ = 4096
DEPTH = 2

FOURIER_GROUPS = 4
FOURIER_GROUP_DIM = D_MODEL // 8
FOURIER_WIDTH = FOURIER_GROUPS * FOURIER_GROUP_DIM
DIFF_HEADS = 4
DIFF_QK_DIM = D_MODEL // 16
DIFF_V_DIM = 2 * DIFF_QK_DIM
DIFF_WIDTH = DIFF_HEADS * DIFF_V_DIM
DIFF_QK_WIDTH = DIFF_HEADS * 2 * DIFF_QK_DIM
EVEN_IN_WIDTH = FOURIER_WIDTH + 2 * DIFF_QK_WIDTH + DIFF_WIDTH
EVEN_MIX_WIDTH = FOURIER_WIDTH + DIFF_WIDTH
Q_BLOCK = 128
REL_BUCKETS = 32
REL_MAX_DIST = 128
SGU_CHUNK = 128
SGU_GROUPS = 8
SGU_WIDTH = D_MODEL
SGU_GROUP_DIM = SGU_WIDTH // SGU_GROUPS
D_FF = ((-(-8 * D_MODEL // 3) + 255) // 256) * 256
RMS_EPS = 1e-6
N_EVEN = (DEPTH + 1) // 2
N_ODD = DEPTH // 2

kernel_name = 'hybrid_fourier_diffattn_sgu_encoder'


def rms_norm(x, g):
    xf = x.astype(jnp.float32)
    y = xf * lax.rsqrt(jnp.mean(xf * xf, axis=-1, keepdims=True) + RMS_EPS)
    return (y * g.astype(jnp.float32)).astype(x.dtype)


def t5_bucket(rel):
    half = REL_BUCKETS // 2
    max_exact = half // 2
    ret = (rel > 0).astype(jnp.int32) * half
    n = jnp.abs(rel)
    nf = jnp.maximum(n, 1).astype(jnp.float32)
    large = max_exact + (jnp.log(nf / max_exact) / math.log(REL_MAX_DIST / max_exact)
                         * (half - max_exact)).astype(jnp.int32)
    large = jnp.minimum(large, half - 1)
    return ret + jnp.where(n < max_exact, n, large)


def fourier_mix(z):
    b, s, _ = z.shape
    zg = z.astype(jnp.float32).reshape(b, s, FOURIER_GROUPS, FOURIER_GROUP_DIM)
    f = jnp.fft.fftn(zg, axes=(1, 3), norm='ortho').real
    return f.reshape(b, s, FOURIER_WIDTH).astype(z.dtype)


def diff_attention(q, k, v, rel_table, lam, lam_init, subln_g):
    b, s = q.shape[0], q.shape[1]
    nb = s // Q_BLOCK
    qb = (q * (DIFF_QK_DIM ** -0.5)).reshape(b, nb, Q_BLOCK, DIFF_HEADS, 2, DIFF_QK_DIM)
    qb = jnp.moveaxis(qb, 1, 0)
    kpos = jnp.arange(s, dtype=jnp.int32)

    def one_block(args):
        qi, bi = args
        qpos = bi * Q_BLOCK + jnp.arange(Q_BLOCK, dtype=jnp.int32)
        bias = rel_table[t5_bucket(kpos[None, :] - qpos[:, None])]
        bias = jnp.transpose(bias, (2, 0, 1)).astype(jnp.float32)
        logits = jnp.einsum('bqhjd,bkhjd->bhjqk', qi, k).astype(jnp.float32)
        p = jax.nn.softmax(logits + bias[None, :, None], axis=-1)
        a = p[:, :, 0] - lam * p[:, :, 1]
        return jnp.einsum('bhqk,bkhe->bqhe', a.astype(v.dtype), v)

    o = lax.map(one_block, (qb, jnp.arange(nb, dtype=jnp.int32)))
    o = jnp.moveaxis(o, 0, 1).reshape(b, s, DIFF_HEADS, DIFF_V_DIM)
    o = rms_norm(o, subln_g) * (1.0 - lam_init)
    return o.reshape(b, s, DIFF_WIDTH)


def even_mixer(h, w_in, w_out, rel_table, lambdas, subln_g, lam_init):
    b, s, _ = h.shape
    z = h @ w_in
    zf = z[..., :FOURIER_WIDTH]
    zq = z[..., FOURIER_WIDTH:FOURIER_WIDTH + DIFF_QK_WIDTH]
    zk = z[..., FOURIER_WIDTH + DIFF_QK_WIDTH:FOURIER_WIDTH + 2 * DIFF_QK_WIDTH]
    zv = z[..., FOURIER_WIDTH + 2 * DIFF_QK_WIDTH:]
    f = fourier_mix(zf)
    q = zq.reshape(b, s, DIFF_HEADS, 2, DIFF_QK_DIM)
    k = zk.reshape(b, s, DIFF_HEADS, 2, DIFF_QK_DIM)
    v = zv.reshape(b, s, DIFF_HEADS, DIFF_V_DIM)
    lf = lambdas.astype(jnp.float32)
    lam = jnp.exp(jnp.sum(lf[0] * lf[1])) - jnp.exp(jnp.sum(lf[2] * lf[3])) + lam_init
    a = diff_attention(q, k, v, rel_table, lam, lam_init, subln_g)
    return jnp.concatenate([f, a], axis=-1) @ w_out


def odd_mixer(h, w_uv, v_norm_g, w_s, b_s, w_out):
    b, s, _ = h.shape
    z = jax.nn.gelu(h @ w_uv, approximate=False)
    u = z[..., :SGU_WIDTH]
    v = rms_norm(z[..., SGU_WIDTH:], v_norm_g)
    nc = s // SGU_CHUNK
    vc = v.reshape(b, nc, SGU_CHUNK, SGU_GROUPS, SGU_GROUP_DIM)
    sv = jnp.einsum('gpq,bnqgc->bnpgc', w_s, vc) + jnp.transpose(b_s)[None, None, :, :, None]
    y = u * sv.reshape(b, s, SGU_WIDTH)
    return y @ w_out


def swiglu(h, w1, w3, w2):
    return (jax.nn.silu(h @ w1) * (h @ w3)) @ w2


def setup_inputs(seed: int = 0) -> dict:
    key = jax.random.key(seed)
    ks = jax.random.split(key, 20)
    f32 = jnp.float32
    nrm = lambda k, shape, scale: jax.random.normal(k, shape, f32) * scale
    gain = lambda k, shape: 1.0 + 0.02 * jax.random.normal(k, shape, f32)
    return {
        'x': jax.random.normal(ks[0], (BATCH, SEQ, D_MODEL), f32),
        'rel_bias_table': nrm(ks[1], (REL_BUCKETS, DIFF_HEADS), 0.5),
        'norm_mix_g': gain(ks[2], (DEPTH, D_MODEL)),
        'norm_ffn_g': gain(ks[3], (DEPTH, D_MODEL)),
        'even_w_in': nrm(ks[4], (N_EVEN, D_MODEL, EVEN_IN_WIDTH), D_MODEL ** -0.5),
        'even_w_out': nrm(ks[5], (N_EVEN, EVEN_MIX_WIDTH, D_MODEL), EVEN_MIX_WIDTH ** -0.5),
        'diff_lambda': nrm(ks[6], (N_EVEN, 4, DIFF_QK_DIM), 0.1),
        'diff_subln_g': gain(ks[7], (N_EVEN, DIFF_V_DIM)),
        'odd_w_uv': nrm(ks[8], (N_ODD, D_MODEL, 2 * SGU_WIDTH), D_MODEL ** -0.5),
        'odd_v_norm_g': gain(ks[9], (N_ODD, SGU_WIDTH)),
        'odd_w_s': nrm(ks[10], (N_ODD, SGU_GROUPS, SGU_CHUNK, SGU_CHUNK), SGU_CHUNK ** -0.5),
        'odd_b_s': gain(ks[11], (N_ODD, SGU_GROUPS, SGU_CHUNK)),
        'odd_w_out': nrm(ks[12], (N_ODD, SGU_WIDTH, D_MODEL), SGU_WIDTH ** -0.5),
        'ffn_w1': nrm(ks[13], (DEPTH, D_MODEL, D_FF), D_MODEL ** -0.5),
        'ffn_w3': nrm(ks[14], (DEPTH, D_MODEL, D_FF), D_MODEL ** -0.5),
        'ffn_w2': nrm(ks[15], (DEPTH, D_FF, D_MODEL), D_FF ** -0.5),
        'final_norm_g': gain(ks[16], (D_MODEL,)),
    }


def reference(x, rel_bias_table, norm_mix_g, norm_ffn_g, even_w_in, even_w_out,
              diff_lambda, diff_subln_g, odd_w_uv, odd_v_norm_g, odd_w_s, odd_b_s,
              odd_w_out, ffn_w1, ffn_w3, ffn_w2, final_norm_g):
    h = x
    for i in range(DEPTH):
        hn = rms_norm(h, norm_mix_g[i])
        j = i // 2
        if i % 2 == 0:
            lam_init = 0.8 - 0.6 * math.exp(-0.3 * i)
            m = even_mixer(hn, even_w_in[j], even_w_out[j], rel_bias_table,
                           diff_lambda[j], diff_subln_g[j], lam_init)
        else:
            m = odd_mixer(hn, odd_w_uv[j], odd_v_norm_g[j], odd_w_s[j], odd_b_s[j],
                          odd_w_out[j])
        h = h + m
        hn = rms_norm(h, norm_ffn_g[i])
        h = h + swiglu(hn, ffn_w1[i], ffn_w3[i], ffn_w2[i])
    return rms_norm(h, final_norm_g)
```

```python
import functools
import math

import numpy as np
import jax
import jax.numpy as jnp
from jax import lax
from jax.experimental import pallas as pl
from jax.experimental.pallas import tpu as pltpu

F32 = jnp.float32
BF16 = jnp.bfloat16

D_MODEL = 1024
FOURIER_GROUP_DIM = 128
FOURIER_WIDTH = 512
DIFF_HEADS = 4
DIFF_QK_DIM = 64
DIFF_V_DIM = 128
DIFF_WIDTH = 512
REL_BUCKETS = 32
REL_MAX_DIST = 128
SGU_CHUNK = 128
SGU_GROUPS = 8
SGU_WIDTH = 1024
D_FF = 2816
RMS_EPS = 1e-6

DFT_RADIX = 64
SEQ_LEN = DFT_RADIX * DFT_RADIX

ROW_TILE = 512
ATTN_TILE = 512
FF_CHUNK = 256
DFT1_GROUP = 8
DFT2_LANES = 8192
VMEM_LIMIT = 56 * 1024 * 1024


def _rms(x, g):
    return x * lax.rsqrt(jnp.mean(x * x, axis=-1, keepdims=True) + RMS_EPS) * g


def _const_spec(shape):
    return pl.BlockSpec(shape, lambda *_: (0,) * len(shape))


def _dft_constants():
    r = DFT_RADIX
    idx = np.arange(r)
    ang = 2.0 * np.pi * np.outer(idx, idx) / r
    c, s = np.cos(ang), np.sin(ang)
    a1 = np.block([[c, -s], [-s, -c]])
    scale = 1.0 / math.sqrt(SEQ_LEN * FOURIER_GROUP_DIM)
    a2 = np.concatenate([c, s], axis=1) * scale
    tang = 2.0 * np.pi * np.outer(idx, idx) / SEQ_LEN
    twr = np.broadcast_to(np.cos(tang)[:, :, None], (r, r, 128))
    twi = np.broadcast_to(np.sin(tang)[:, :, None], (r, r, 128))
    cidx = np.arange(FOURIER_GROUP_DIM)
    cang = 2.0 * np.pi * np.outer(cidx, cidx) / FOURIER_GROUP_DIM
    eye2 = np.eye(2)
    cc = np.kron(eye2, np.cos(cang))
    cs = np.kron(eye2, np.sin(cang))
    return (jnp.asarray(a1, BF16), jnp.asarray(a2, BF16),
            jnp.asarray(twr, F32), jnp.asarray(twi, F32),
            jnp.asarray(np.stack([cc, cs]), BF16))


def _even_in_kernel(x_ref, g_ref, w_ref, cdft_ref, xcs_ref, q_ref, k_ref, v_ref):
    hn = _rms(x_ref[...], g_ref[...]).astype(BF16)
    z = jnp.dot(hn, w_ref[...], preferred_element_type=F32)
    zf = z[:, :FOURIER_WIDTH].astype(BF16)
    for p in range(FOURIER_WIDTH // 256):
        zp = zf[:, p * 256:(p + 1) * 256]
        xcs_ref[0, 0, :, p * 256:(p + 1) * 256] = jnp.dot(
            zp, cdft_ref[0], preferred_element_type=F32).astype(BF16)
        xcs_ref[0, 1, :, p * 256:(p + 1) * 256] = jnp.dot(
            zp, cdft_ref[1], preferred_element_type=F32).astype(BF16)
    o = FOURIER_WIDTH
    q_ref[...] = (z[:, o:o + 512] * (DIFF_QK_DIM ** -0.5)).astype(BF16)
    k_ref[...] = z[:, o + 512:o + 1024].astype(BF16)
    v_ref[...] = z[:, o + 1024:o + 1536].astype(BF16)


def _even_in(x, g, w, cdft, batch, seq):
    m = x.shape[0]
    tm = ROW_TILE
    per_b = seq // tm
    n_out = w.shape[1]
    return pl.pallas_call(
        _even_in_kernel,
        grid=(m // tm,),
        in_specs=[
            pl.BlockSpec((tm, D_MODEL), lambda i: (i, 0)),
            _const_spec((1, D_MODEL)),
            _const_spec((D_MODEL, n_out)),
            _const_spec((2, 256, 256)),
        ],
        out_specs=[
            pl.BlockSpec((1, 2, tm, FOURIER_WIDTH), lambda i: (i // per_b, 0, i % per_b, 0)),
            pl.BlockSpec((tm, 512), lambda i: (i, 0)),
            pl.BlockSpec((tm, 512), lambda i: (i, 0)),
            pl.BlockSpec((tm, 512), lambda i: (i, 0)),
        ],
        out_shape=[
            jax.ShapeDtypeStruct((batch, 2, seq, FOURIER_WIDTH), BF16),
            jax.ShapeDtypeStruct((m, 512), BF16),
            jax.ShapeDtypeStruct((m, 512), BF16),
            jax.ShapeDtypeStruct((m, 512), BF16),
        ],
        compiler_params=pltpu.CompilerParams(
            dimension_semantics=("arbitrary",), vmem_limit_bytes=VMEM_LIMIT),
        name="even_in_proj",
    )(x, g, w, cdft)


def _dft1_kernel(x_ref, a1_ref, twr_ref, twi_ref, y_ref):
    r = DFT_RADIX
    y = jnp.dot(a1_ref[...], x_ref[0], preferred_element_type=F32)
    for i in range(DFT1_GROUP):
        c = twr_ref[i]
        s = twi_ref[i]
        for jj in range(FOURIER_WIDTH // 128):
            lo = i * FOURIER_WIDTH + jj * 128
            yr = y[:r, lo:lo + 128]
            yi = y[r:, lo:lo + 128]
            y_ref[0, 0, i, :, jj * 128:(jj + 1) * 128] = (yr * c + yi * s).astype(BF16)
            y_ref[0, 1, i, :, jj * 128:(jj + 1) * 128] = (yi * c - yr * s).astype(BF16)


def _dft2_kernel(y_ref, a2_ref, f_ref):
    f_ref[0] = jnp.dot(a2_ref[...], y_ref[0], preferred_element_type=F32).astype(BF16)


def _fourier(xcs, a1, a2, twr, twi, batch):
    r = DFT_RADIX
    lanes = r * FOURIER_WIDTH
    g = DFT1_GROUP
    x2 = xcs.reshape(batch, 2 * r, lanes)
    yt = pl.pallas_call(
        _dft1_kernel,
        grid=(batch, r // g),
        in_specs=[
            pl.BlockSpec((1, 2 * r, g * FOURIER_WIDTH), lambda b, j: (b, 0, j)),
            _const_spec((2 * r, 2 * r)),
            pl.BlockSpec((g, r, 128), lambda b, j: (j, 0, 0)),
            pl.BlockSpec((g, r, 128), lambda b, j: (j, 0, 0)),
        ],
        out_specs=pl.BlockSpec((1, 2, g, r, FOURIER_WIDTH), lambda b, j: (b, 0, j, 0, 0)),
        out_shape=jax.ShapeDtypeStruct((batch, 2, r, r, FOURIER_WIDTH), BF16),
        compiler_params=pltpu.CompilerParams(
            dimension_semantics=("arbitrary", "arbitrary"), vmem_limit_bytes=VMEM_LIMIT),
        name="seq_dft_stage1",
    )(x2, a1, twr, twi)
    y2 = yt.reshape(batch, 2 * r, lanes)
    f = pl.pallas_call(
        _dft2_kernel,
        grid=(batch, lanes // DFT2_LANES),
        in_specs=[
            pl.BlockSpec((1, 2 * r, DFT2_LANES), lambda b, j: (b, 0, j)),
            _const_spec((r, 2 * r)),
        ],
        out_specs=pl.BlockSpec((1, r, DFT2_LANES), lambda b, j: (b, 0, j)),
        out_shape=jax.ShapeDtypeStruct((batch, r, lanes), BF16),
        compiler_params=pltpu.CompilerParams(
            dimension_semantics=("arbitrary", "arbitrary"), vmem_limit_bytes=VMEM_LIMIT),
        name="seq_dft_stage2",
    )(y2, a2)
    return f.reshape(batch * SEQ_LEN, FOURIER_WIDTH)


def _bias_tiles_kernel(tab_ref, o_ref):
    h = pl.program_id(0)
    t = pl.program_id(1)
    tt = ATTN_TILE
    half = REL_BUCKETS // 2
    max_exact = half // 2
    row = lax.broadcasted_iota(jnp.int32, (tt, tt), 0)
    col = lax.broadcasted_iota(jnp.int32, (tt, tt), 1)
    rel = col - row + (t - 1) * tt
    ret = jnp.where(rel > 0, half, 0)
    n = jnp.abs(rel)
    nf = jnp.maximum(n, 1).astype(F32)
    large = max_exact + (jnp.log(nf / max_exact) / math.log(REL_MAX_DIST / max_exact)
                         * (half - max_exact)).astype(jnp.int32)
    large = jnp.minimum(large, half - 1)
    bucket = ret + jnp.where(n < max_exact, n, large)
    acc = jnp.zeros((tt, tt), F32)
    for bk in range(REL_BUCKETS):
        acc = jnp.where(bucket == bk, tab_ref[bk, h], acc)
    o_ref[0, 0] = acc


def _bias_tiles(table):
    tt = ATTN_TILE
    return pl.pallas_call(
        _bias_tiles_kernel,
        grid=(DIFF_HEADS, 3),
        in_specs=[pl.BlockSpec(memory_space=pltpu.SMEM)],
        out_specs=pl.BlockSpec((1, 1, tt, tt), lambda h, t: (h, t, 0, 0)),
        out_shape=jax.ShapeDtypeStruct((DIFF_HEADS, 3, tt, tt), F32),
        compiler_params=pltpu.CompilerParams(
            dimension_semantics=("arbitrary", "arbitrary"), vmem_limit_bytes=VMEM_LIMIT),
        name="rel_bias_tiles",
    )(table)


def _attn_kernel(tab_ref, q_ref, k_ref, v_ref, bias_ref, lam_ref, g_ref, o_ref,
                 qm_ref, m_ref, l_ref, acc_ref, *, lam_init, n_kv):
    h = pl.program_id(1)
    i = pl.program_id(2)
    j = pl.program_id(3)
    half = REL_BUCKETS // 2

    @pl.when(j == 0)
    def _init():
        q = q_ref[...].astype(F32)
        lane = lax.broadcasted_iota(jnp.int32, q.shape, 1)
        qm_ref[0] = jnp.where(lane < DIFF_QK_DIM, q, 0.0).astype(BF16)
        qm_ref[1] = jnp.where(lane >= DIFF_QK_DIM, q, 0.0).astype(BF16)
        m_ref[...] = jnp.full(m_ref.shape, -jnp.inf, F32)
        l_ref[...] = jnp.zeros(l_ref.shape, F32)
        acc_ref[...] = jnp.zeros(acc_ref.shape, F32)

    def step(bias_tile, const):
        kt = k_ref[...]
        vt = v_ref[...]
        for mp in range(2):
            s = lax.dot_general(qm_ref[mp], kt, (((1,), (1,)), ((), ())),
                                preferred_element_type=F32)
            if bias_tile is not None:
                s = s + bias_tile
            m_old = m_ref[mp]
            rmax = jnp.max(s, axis=-1, keepdims=True)
            if const is not None:
                rmax = rmax + const
            m_new = jnp.maximum(m_old, rmax)
            alpha = jnp.exp(m_old - m_new)
            shift = m_new if const is None else m_new - const
            p = jnp.exp(s - shift)
            l_ref[mp] = alpha * l_ref[mp] + jnp.sum(p, axis=-1, keepdims=True)
            acc_ref[mp] = alpha * acc_ref[mp] + jnp.dot(
                p.astype(BF16), vt, preferred_element_type=F32)
            m_ref[mp] = m_new

    d = j - i
    near = jnp.abs(d) <= 1

    @pl.when(near)
    def _near():
        step(bias_ref[0, d + 1], None)

    @pl.when(jnp.logical_not(near))
    def _far():
        const = jnp.where(d > 0, tab_ref[REL_BUCKETS - 1, h], tab_ref[half - 1, h])
        step(None, const)

    @pl.when(j == n_kv - 1)
    def _finalize():
        lf = lam_ref[...]
        lam = (jnp.exp(jnp.sum(lf[0:1] * lf[1:2], axis=-1, keepdims=True))
               - jnp.exp(jnp.sum(lf[2:3] * lf[3:4], axis=-1, keepdims=True)) + lam_init)
        o = acc_ref[0] / l_ref[0] - lam * (acc_ref[1] / l_ref[1])
        o_ref[...] = (_rms(o, g_ref[...]) * (1.0 - lam_init)).astype(BF16)


def _attention(q, k, v, bias, table, lam, subln_g, lam_init, batch, seq):
    tt = ATTN_TILE
    nq = seq // tt
    m = batch * seq
    kern = functools.partial(_attn_kernel, lam_init=lam_init, n_kv=nq)
    return pl.pallas_call(
        kern,
        grid=(batch, DIFF_HEADS, nq, nq),
        in_specs=[
            pl.BlockSpec(memory_space=pltpu.SMEM),
            pl.BlockSpec((tt, DIFF_V_DIM), lambda b, h, i, j: (b * nq + i, h)),
            pl.BlockSpec((tt, DIFF_V_DIM), lambda b, h, i, j: (b * nq + j, h)),
            pl.BlockSpec((tt, DIFF_V_DIM), lambda b, h, i, j: (b * nq + j, h)),
            pl.BlockSpec((1, 3, tt, tt), lambda b, h, i, j: (h, 0, 0, 0)),
            _const_spec((4, DIFF_QK_DIM)),
            _const_spec((1, DIFF_V_DIM)),
        ],
        out_specs=pl.BlockSpec((tt, DIFF_V_DIM), lambda b, h, i, j: (b * nq + i, h)),
        out_shape=jax.ShapeDtypeStruct((m, DIFF_WIDTH), BF16),
        scratch_shapes=[
            pltpu.VMEM((2, tt, DIFF_V_DIM), BF16),
            pltpu.VMEM((2, tt, 1), F32),
            pltpu.VMEM((2, tt, 1), F32),
            pltpu.VMEM((2, tt, DIFF_V_DIM), F32),
        ],
        compiler_params=pltpu.CompilerParams(
            dimension_semantics=("arbitrary", "arbitrary", "arbitrary", "arbitrary"),
            vmem_limit_bytes=VMEM_LIMIT),
        name="diff_attention",
    )(table, q, k, v, bias, lam, subln_g)


def _even_out_kernel(x_ref, f_ref, a_ref, wf_ref, wa_ref, o_ref):
    o_ref[...] = (x_ref[...]
                  + jnp.dot(f_ref[...], wf_ref[...], preferred_element_type=F32)
                  + jnp.dot(a_ref[...], wa_ref[...], preferred_element_type=F32))


def _even_out(x, f, a, wf, wa):
    m = x.shape[0]
    tm = ROW_TILE
    return pl.pallas_call(
        _even_out_kernel,
        grid=(m // tm,),
        in_specs=[
            pl.BlockSpec((tm, D_MODEL), lambda i: (i, 0)),
            pl.BlockSpec((tm, FOURIER_WIDTH), lambda i: (i, 0)),
            pl.BlockSpec((tm, DIFF_WIDTH), lambda i: (i, 0)),
            _const_spec((FOURIER_WIDTH, D_MODEL)),
            _const_spec((DIFF_WIDTH, D_MODEL)),
        ],
        out_specs=pl.BlockSpec((tm, D_MODEL), lambda i: (i, 0)),
        out_shape=jax.ShapeDtypeStruct((m, D_MODEL), F32),
        compiler_params=pltpu.CompilerParams(
            dimension_semantics=("arbitrary",), vmem_limit_bytes=VMEM_LIMIT),
        name="even_out_proj",
    )(x, f, a, wf, wa)


def _ffn_kernel(x_ref, g_ref, w1_ref, w3_ref, w2_ref, gf_ref, o_ref, *, final_norm):
    x = x_ref[...]
    hn = _rms(x, g_ref[...]).astype(BF16)
    acc = x
    for c in range(D_FF // FF_CHUNK):
        a = jnp.dot(hn, w1_ref[c], preferred_element_type=F32)
        b = jnp.dot(hn, w3_ref[c], preferred_element_type=F32)
        u = (a * jax.nn.sigmoid(a) * b).astype(BF16)
        acc = acc + jnp.dot(u, w2_ref[c], preferred_element_type=F32)
    if final_norm:
        acc = _rms(acc, gf_ref[...])
    o_ref[...] = acc


def _ffn(x, g, w1, w3, w2, gf, final_norm):
    m = x.shape[0]
    tm = ROW_TILE
    nc = D_FF // FF_CHUNK
    return pl.pallas_call(
        functools.partial(_ffn_kernel, final_norm=final_norm),
        grid=(m // tm,),
        in_specs=[
            pl.BlockSpec((tm, D_MODEL), lambda i: (i, 0)),
            _const_spec((1, D_MODEL)),
            _const_spec((nc, D_MODEL, FF_CHUNK)),
            _const_spec((nc, D_MODEL, FF_CHUNK)),
            _const_spec((nc, FF_CHUNK, D_MODEL)),
            _const_spec((1, D_MODEL)),
        ],
        out_specs=pl.BlockSpec((tm, D_MODEL), lambda i: (i, 0)),
        out_shape=jax.ShapeDtypeStruct((m, D_MODEL), F32),
        compiler_params=pltpu.CompilerParams(
            dimension_semantics=("arbitrary",), vmem_limit_bytes=VMEM_LIMIT),
        name="swiglu_ffn",
    )(x, g, w1, w3, w2, gf)


def _odd_kernel(x_ref, g_ref, wuv_ref, gv_ref, ws_ref, bs_ref, wo_ref, o_ref, y_ref):
    x = x_ref[...]
    hn = _rms(x, g_ref[...]).astype(BF16)
    z = jnp.dot(hn, wuv_ref[...], preferred_element_type=F32)
    z = 0.5 * z * (1.0 + lax.erf(z * math.sqrt(0.5)))
    u = z[:, :SGU_WIDTH]
    vn = _rms(z[:, SGU_WIDTH:], gv_ref[...]).astype(BF16)
    gd = SGU_WIDTH // SGU_GROUPS
    for r in range(x.shape[0] // SGU_CHUNK):
        rows = slice(r * SGU_CHUNK, (r + 1) * SGU_CHUNK)
        for g in range(SGU_GROUPS):
            cols = slice(g * gd, (g + 1) * gd)
            sv = jnp.dot(ws_ref[g], vn[rows, cols], preferred_element_type=F32) + bs_ref[g]
            y_ref[rows, cols] = (u[rows, cols] * sv).astype(BF16)
    o_ref[...] = x + jnp.dot(y_ref[...], wo_ref[...], preferred_element_type=F32)


def _odd_mixer(x, g, wuv, gv, ws, bs, wo):
    m = x.shape[0]
    tm = ROW_TILE
    gd = SGU_WIDTH // SGU_GROUPS
    return pl.pallas_call(
        _odd_kernel,
        grid=(m // tm,),
        in_specs=[
            pl.BlockSpec((tm, D_MODEL), lambda i: (i, 0)),
            _const_spec((1, D_MODEL)),
            _const_spec((D_MODEL, 2 * SGU_WIDTH)),
            _const_spec((1, SGU_WIDTH)),
            _const_spec((SGU_GROUPS, SGU_CHUNK, SGU_CHUNK)),
            _const_spec((SGU_GROUPS, SGU_CHUNK, gd)),
            _const_spec((SGU_WIDTH, D_MODEL)),
        ],
        out_specs=pl.BlockSpec((tm, D_MODEL), lambda i: (i, 0)),
        out_shape=jax.ShapeDtypeStruct((m, D_MODEL), F32),
        scratch_shapes=[pltpu.VMEM((tm, SGU_WIDTH), BF16)],
        compiler_params=pltpu.CompilerParams(
            dimension_semantics=("arbitrary",), vmem_limit_bytes=VMEM_LIMIT),
        name="sgu_mixer",
    )(x, g, wuv, gv, ws, bs, wo)


def _chunk_cols(w):
    k, n = w.shape
    return jnp.transpose(w.reshape(k, n // FF_CHUNK, FF_CHUNK), (1, 0, 2))


def kernel(x, rel_bias_table, norm_mix_g, norm_ffn_g, even_w_in, even_w_out, diff_lambda,
           diff_subln_g, odd_w_uv, odd_v_norm_g, odd_w_s, odd_b_s, odd_w_out, ffn_w1, ffn_w3,
           ffn_w2, final_norm_g):
    batch, seq, d = x.shape
    assert (seq, d) == (SEQ_LEN, D_MODEL)
    depth = norm_mix_g.shape[0]
    m = batch * seq
    h = x.reshape(m, d)
    a1, a2, twr, twi, cdft = _dft_constants()

    for i in range(depth):
        jx = i // 2
        g_mix = norm_mix_g[i].reshape(1, d)
        if i % 2 == 0:
            lam_init = 0.8 - 0.6 * math.exp(-0.3 * i)
            xcs, q, k, v = _even_in(h, g_mix, even_w_in[jx].astype(BF16), cdft, batch, seq)
            f = _fourier(xcs, a1, a2, twr, twi, batch)
            bias = _bias_tiles(rel_bias_table)
            att = _attention(q, k, v, bias, rel_bias_table, diff_lambda[jx],
                             diff_subln_g[jx].reshape(1, DIFF_V_DIM), lam_init, batch, seq)
            w_out = even_w_out[jx].astype(BF16)
            h = _even_out(h, f, att, w_out[:FOURIER_WIDTH], w_out[FOURIER_WIDTH:])
        else:
            bs = jnp.broadcast_to(odd_b_s[jx][:, :, None],
                                  (SGU_GROUPS, SGU_CHUNK, SGU_WIDTH // SGU_GROUPS))
            h = _odd_mixer(h, g_mix, odd_w_uv[jx].astype(BF16),
                           odd_v_norm_g[jx].reshape(1, SGU_WIDTH),
                           odd_w_s[jx].astype(BF16), bs, odd_w_out[jx].astype(BF16))
        last = i == depth - 1
        h = _ffn(h, norm_ffn_g[i].reshape(1, d),
                 _chunk_cols(ffn_w1[i].astype(BF16)), _chunk_cols(ffn_w3[i].astype(BF16)),
                 ffn_w2[i].astype(BF16).reshape(D_FF // FF_CHUNK, FF_CHUNK, d),
                 final_norm_g.reshape(1, d), last)
    return h.reshape(batch, seq, d)
```

```python
import functools
import math

import numpy as np
import jax
import jax.numpy as jnp
from jax import lax
from jax.experimental import pallas as pl
from jax.experimental.pallas import tpu as pltpu

F32 = jnp.float32
BF16 = jnp.bfloat16

D_MODEL = 1024
FOURIER_GROUP_DIM = 128
FOURIER_WIDTH = 512
DIFF_HEADS = 4
DIFF_QK_DIM = 64
DIFF_V_DIM = 128
DIFF_WIDTH = 512
REL_BUCKETS = 32
REL_MAX_DIST = 128
SGU_CHUNK = 128
SGU_GROUPS = 8
SGU_WIDTH = 1024
D_FF = 2816
RMS_EPS = 1e-6
LOG2E = math.log2(math.e)

DFT_RADIX = 64
SEQ_LEN = DFT_RADIX * DFT_RADIX

ROW_TILE = 512
ATTN_TILE = 512
FF_CHUNK = 256
DFT1_GROUP = 8
DFT2_LANES = 8192
VMEM_LIMIT = 56 * 1024 * 1024


def _rms(x, g):
    return x * lax.rsqrt(jnp.mean(x * x, axis=-1, keepdims=True) + RMS_EPS) * g


def _const_spec(shape):
    return pl.BlockSpec(shape, lambda *_: (0,) * len(shape))


def _dft_constants():
    r = DFT_RADIX
    idx = np.arange(r)
    ang = 2.0 * np.pi * np.outer(idx, idx) / r
    c, s = np.cos(ang), np.sin(ang)
    a1 = np.block([[c, -s], [-s, -c]])
    scale = 1.0 / math.sqrt(SEQ_LEN * FOURIER_GROUP_DIM)
    a2 = np.concatenate([c, s], axis=1) * scale
    tang = 2.0 * np.pi * np.outer(idx, idx) / SEQ_LEN
    twr = np.broadcast_to(np.cos(tang)[:, :, None], (r, r, 128))
    twi = np.broadcast_to(np.sin(tang)[:, :, None], (r, r, 128))
    cidx = np.arange(FOURIER_GROUP_DIM)
    cang = 2.0 * np.pi * np.outer(cidx, cidx) / FOURIER_GROUP_DIM
    eye2 = np.eye(2)
    cc = np.kron(eye2, np.cos(cang))
    cs = np.kron(eye2, np.sin(cang))
    return (jnp.asarray(a1, BF16), jnp.asarray(a2, BF16),
            jnp.asarray(twr, F32), jnp.asarray(twi, F32),
            jnp.asarray(np.stack([cc, cs]), BF16))


def _even_in_kernel(x_ref, g_ref, w_ref, wvt_ref, cdft_ref, xcs_ref, q_ref, k_ref, vt_ref):
    hn = _rms(x_ref[...], g_ref[...]).astype(BF16)
    z = jnp.dot(hn, w_ref[...], preferred_element_type=F32)
    vt = lax.dot_general(wvt_ref[...], hn, (((1,), (1,)), ((), ())),
                         preferred_element_type=F32)
    for hh in range(DIFF_HEADS):
        vt_ref[0, hh, 0] = vt[hh * DIFF_V_DIM:(hh + 1) * DIFF_V_DIM].astype(BF16)
    zf = z[:, :FOURIER_WIDTH].astype(BF16)
    for p in range(FOURIER_WIDTH // 256):
        zp = zf[:, p * 256:(p + 1) * 256]
        xcs_ref[0, 0, :, p * 256:(p + 1) * 256] = jnp.dot(
            zp, cdft_ref[0], preferred_element_type=F32).astype(BF16)
        xcs_ref[0, 1, :, p * 256:(p + 1) * 256] = jnp.dot(
            zp, cdft_ref[1], preferred_element_type=F32).astype(BF16)
    o = FOURIER_WIDTH
    q_ref[...] = (z[:, o:o + 512] * (DIFF_QK_DIM ** -0.5 * LOG2E)).astype(BF16)
    k_ref[...] = z[:, o + 512:o + 1024].astype(BF16)


def _even_in(x, g, w, wvt, cdft, batch, seq):
    m = x.shape[0]
    tm = ROW_TILE
    assert tm == ATTN_TILE
    per_b = seq // tm
    n_out = w.shape[1]
    return pl.pallas_call(
        _even_in_kernel,
        grid=(m // tm,),
        in_specs=[
            pl.BlockSpec((tm, D_MODEL), lambda i: (i, 0)),
            _const_spec((1, D_MODEL)),
            _const_spec((D_MODEL, n_out)),
            _const_spec((DIFF_WIDTH, D_MODEL)),
            _const_spec((2, 256, 256)),
        ],
        out_specs=[
            pl.BlockSpec((1, 2, tm, FOURIER_WIDTH), lambda i: (i // per_b, 0, i % per_b, 0)),
            pl.BlockSpec((tm, 512), lambda i: (i, 0)),
            pl.BlockSpec((tm, 512), lambda i: (i, 0)),
            pl.BlockSpec((1, DIFF_HEADS, 1, DIFF_V_DIM, tm),
                         lambda i: (i // per_b, 0, i % per_b, 0, 0)),
        ],
        out_shape=[
            jax.ShapeDtypeStruct((batch, 2, seq, FOURIER_WIDTH), BF16),
            jax.ShapeDtypeStruct((m, 512), BF16),
            jax.ShapeDtypeStruct((m, 512), BF16),
            jax.ShapeDtypeStruct((batch, DIFF_HEADS, per_b, DIFF_V_DIM, tm), BF16),
        ],
        compiler_params=pltpu.CompilerParams(
            dimension_semantics=("arbitrary",), vmem_limit_bytes=VMEM_LIMIT),
        name="even_in_proj",
    )(x, g, w, wvt, cdft)


def _dft1_kernel(x_ref, a1_ref, twr_ref, twi_ref, y_ref):
    r = DFT_RADIX
    y = jnp.dot(a1_ref[...], x_ref[0], preferred_element_type=F32)
    for i in range(DFT1_GROUP):
        c = twr_ref[i]
        s = twi_ref[i]
        for jj in range(FOURIER_WIDTH // 128):
            lo = i * FOURIER_WIDTH + jj * 128
            yr = y[:r, lo:lo + 128]
            yi = y[r:, lo:lo + 128]
            y_ref[0, 0, i, :, jj * 128:(jj + 1) * 128] = (yr * c + yi * s).astype(BF16)
            y_ref[0, 1, i, :, jj * 128:(jj + 1) * 128] = (yi * c - yr * s).astype(BF16)


def _dft2_kernel(y_ref, a2_ref, f_ref):
    f_ref[0] = jnp.dot(a2_ref[...], y_ref[0], preferred_element_type=F32).astype(BF16)


def _fourier(xcs, a1, a2, twr, twi, batch):
    r = DFT_RADIX
    lanes = r * FOURIER_WIDTH
    g = DFT1_GROUP
    x2 = xcs.reshape(batch, 2 * r, lanes)
    yt = pl.pallas_call(
        _dft1_kernel,
        grid=(batch, r // g),
        in_specs=[
            pl.BlockSpec((1, 2 * r, g * FOURIER_WIDTH), lambda b, j: (b, 0, j)),
            _const_spec((2 * r, 2 * r)),
            pl.BlockSpec((g, r, 128), lambda b, j: (j, 0, 0)),
            pl.BlockSpec((g, r, 128), lambda b, j: (j, 0, 0)),
        ],
        out_specs=pl.BlockSpec((1, 2, g, r, FOURIER_WIDTH), lambda b, j: (b, 0, j, 0, 0)),
        out_shape=jax.ShapeDtypeStruct((batch, 2, r, r, FOURIER_WIDTH), BF16),
        compiler_params=pltpu.CompilerParams(
            dimension_semantics=("arbitrary", "arbitrary"), vmem_limit_bytes=VMEM_LIMIT),
        name="seq_dft_stage1",
    )(x2, a1, twr, twi)
    y2 = yt.reshape(batch, 2 * r, lanes)
    f = pl.pallas_call(
        _dft2_kernel,
        grid=(batch, lanes // DFT2_LANES),
        in_specs=[
            pl.BlockSpec((1, 2 * r, DFT2_LANES), lambda b, j: (b, 0, j)),
            _const_spec((r, 2 * r)),
        ],
        out_specs=pl.BlockSpec((1, r, DFT2_LANES), lambda b, j: (b, 0, j)),
        out_shape=jax.ShapeDtypeStruct((batch, r, lanes), BF16),
        compiler_params=pltpu.CompilerParams(
            dimension_semantics=("arbitrary", "arbitrary"), vmem_limit_bytes=VMEM_LIMIT),
        name="seq_dft_stage2",
    )(y2, a2)
    return f.reshape(batch * SEQ_LEN, FOURIER_WIDTH)


def _bias_tiles_kernel(tab_ref, o_ref):
    h = pl.program_id(0)
    tt = ATTN_TILE
    half = REL_BUCKETS // 2
    max_exact = half // 2
    row = lax.broadcasted_iota(jnp.int32, (tt, tt), 0)
    col = lax.broadcasted_iota(jnp.int32, (tt, tt), 1)
    for t in range(5):
        rel = row - col + (t - 2) * tt
        ret = jnp.where(rel > 0, half, 0)
        n = jnp.abs(rel)
        nf = jnp.maximum(n, 1).astype(F32)
        large = max_exact + (jnp.log(nf / max_exact) / math.log(REL_MAX_DIST / max_exact)
                             * (half - max_exact)).astype(jnp.int32)
        large = jnp.minimum(large, half - 1)
        bucket = ret + jnp.where(n < max_exact, n, large)
        lo = half if t > 2 else 0
        hi = half if t < 2 else REL_BUCKETS
        acc = jnp.zeros((tt, tt), F32)
        for bk in range(lo, hi):
            acc = jnp.where(bucket == bk, tab_ref[bk, h], acc)
        o_ref[0, t] = acc * LOG2E


def _bias_tiles(table):
    tt = ATTN_TILE
    return pl.pallas_call(
        _bias_tiles_kernel,
        grid=(DIFF_HEADS,),
        in_specs=[pl.BlockSpec(memory_space=pltpu.SMEM)],
        out_specs=pl.BlockSpec((1, 5, tt, tt), lambda h: (h, 0, 0, 0)),
        out_shape=jax.ShapeDtypeStruct((DIFF_HEADS, 5, tt, tt), F32),
        compiler_params=pltpu.CompilerParams(
            dimension_semantics=("arbitrary",), vmem_limit_bytes=VMEM_LIMIT),
        name="rel_bias_tiles",
    )(table)


def _attn_kernel(q_ref, k_ref, vt_ref, bias_ref, lam_ref, g_ref, o_ref, *, lam_init, n_kv):
    i = pl.program_id(2)
    tt = ATTN_TILE
    q = q_ref[...].astype(F32)
    lane = lax.broadcasted_iota(jnp.int32, q.shape, 1)
    qm = [jnp.where(lane < DIFF_QK_DIM, q, 0.0).astype(BF16),
          jnp.where(lane >= DIFF_QK_DIM, q, 0.0).astype(BF16)]
    m = [jnp.full((1, tt), -jnp.inf, F32) for _ in range(2)]
    l = [jnp.zeros((1, tt), F32) for _ in range(2)]
    acc = [jnp.zeros((DIFF_V_DIM, tt), F32) for _ in range(2)]
    for j in range(n_kv):
        bias = bias_ref[0, jnp.clip(j - i, -2, 2) + 2]
        kt = k_ref[j * tt:(j + 1) * tt, :]
        vt = vt_ref[0, 0, j]
        for mp in range(2):
            s = lax.dot_general(kt, qm[mp], (((1,), (1,)), ((), ())),
                                preferred_element_type=F32) + bias
            m_new = jnp.maximum(m[mp], jnp.max(s, axis=0, keepdims=True))
            alpha = jnp.exp2(m[mp] - m_new)
            p = jnp.exp2(s - m_new)
            l[mp] = alpha * l[mp] + jnp.sum(p, axis=0, keepdims=True)
            acc[mp] = alpha * acc[mp] + jnp.dot(vt, p.astype(BF16),
                                                preferred_element_type=F32)
            m[mp] = m_new
    lf = lam_ref[...]
    lam = (jnp.exp(jnp.sum(lf[0:1] * lf[1:2], axis=-1, keepdims=True))
           - jnp.exp(jnp.sum(lf[2:3] * lf[3:4], axis=-1, keepdims=True)) + lam_init)
    o = acc[0] / l[0] - lam * (acc[1] / l[1])
    y = (o * lax.rsqrt(jnp.mean(o * o, axis=0, keepdims=True) + RMS_EPS) * g_ref[...]
         * (1.0 - lam_init))
    o_ref[...] = y.T.astype(BF16)


def _attention(q, k, vt, bias, lam, subln_g, lam_init, batch, seq):
    tt = ATTN_TILE
    nq = seq // tt
    m = batch * seq
    kern = functools.partial(_attn_kernel, lam_init=lam_init, n_kv=nq)
    return pl.pallas_call(
        kern,
        grid=(batch, DIFF_HEADS, nq),
        in_specs=[
            pl.BlockSpec((tt, DIFF_V_DIM), lambda b, h, i: (b * nq + i, h)),
            pl.BlockSpec((seq, DIFF_V_DIM), lambda b, h, i: (b, h)),
            pl.BlockSpec((1, 1, nq, DIFF_V_DIM, tt), lambda b, h, i: (b, h, 0, 0, 0)),
            pl.BlockSpec((1, 5, tt, tt), lambda b, h, i: (h, 0, 0, 0)),
            _const_spec((4, DIFF_QK_DIM)),
            _const_spec((DIFF_V_DIM, 1)),
        ],
        out_specs=pl.BlockSpec((tt, DIFF_V_DIM), lambda b, h, i: (b * nq + i, h)),
        out_shape=jax.ShapeDtypeStruct((m, DIFF_WIDTH), BF16),
        compiler_params=pltpu.CompilerParams(
            dimension_semantics=("arbitrary", "arbitrary", "arbitrary"),
            vmem_limit_bytes=VMEM_LIMIT),
        name="diff_attention",
    )(q, k, vt, bias, lam, subln_g)


def _even_out_kernel(x_ref, f_ref, a_ref, wf_ref, wa_ref, o_ref):
    o_ref[...] = (x_ref[...]
                  + jnp.dot(f_ref[...], wf_ref[...], preferred_element_type=F32)
                  + jnp.dot(a_ref[...], wa_ref[...], preferred_element_type=F32))


def _even_out(x, f, a, wf, wa):
    m = x.shape[0]
    tm = ROW_TILE
    return pl.pallas_call(
        _even_out_kernel,
        grid=(m // tm,),
        in_specs=[
            pl.BlockSpec((tm, D_MODEL), lambda i: (i, 0)),
            pl.BlockSpec((tm, FOURIER_WIDTH), lambda i: (i, 0)),
            pl.BlockSpec((tm, DIFF_WIDTH), lambda i: (i, 0)),
            _const_spec((FOURIER_WIDTH, D_MODEL)),
            _const_spec((DIFF_WIDTH, D_MODEL)),
        ],
        out_specs=pl.BlockSpec((tm, D_MODEL), lambda i: (i, 0)),
        out_shape=jax.ShapeDtypeStruct((m, D_MODEL), F32),
        compiler_params=pltpu.CompilerParams(
            dimension_semantics=("arbitrary",), vmem_limit_bytes=VMEM_LIMIT),
        name="even_out_proj",
    )(x, f, a, wf, wa)


def _ffn_kernel(x_ref, g_ref, w1_ref, w3_ref, w2_ref, gf_ref, o_ref, *, final_norm):
    x = x_ref[...]
    hn = _rms(x, g_ref[...]).astype(BF16)
    acc = x
    for c in range(D_FF // FF_CHUNK):
        a = jnp.dot(hn, w1_ref[c], preferred_element_type=F32)
        b = jnp.dot(hn, w3_ref[c], preferred_element_type=F32)
        u = (a * jax.nn.sigmoid(a) * b).astype(BF16)
        acc = acc + jnp.dot(u, w2_ref[c], preferred_element_type=F32)
    if final_norm:
        acc = _rms(acc, gf_ref[...])
    o_ref[...] = acc


def _ffn(x, g, w1, w3, w2, gf, final_norm):
    m = x.shape[0]
    tm = ROW_TILE
    nc = D_FF // FF_CHUNK
    return pl.pallas_call(
        functools.partial(_ffn_kernel, final_norm=final_norm),
        grid=(m // tm,),
        in_specs=[
            pl.BlockSpec((tm, D_MODEL), lambda i: (i, 0)),
            _const_spec((1, D_MODEL)),
            _const_spec((nc, D_MODEL, FF_CHUNK)),
            _const_spec((nc, D_MODEL, FF_CHUNK)),
            _const_spec((nc, FF_CHUNK, D_MODEL)),
            _const_spec((1, D_MODEL)),
        ],
        out_specs=pl.BlockSpec((tm, D_MODEL), lambda i: (i, 0)),
        out_shape=jax.ShapeDtypeStruct((m, D_MODEL), F32),
        compiler_params=pltpu.CompilerParams(
            dimension_semantics=("arbitrary",), vmem_limit_bytes=VMEM_LIMIT),
        name="swiglu_ffn",
    )(x, g, w1, w3, w2, gf)


def _odd_kernel(x_ref, g_ref, wuv_ref, gv_ref, ws_ref, bs_ref, wo_ref, o_ref, y_ref):
    x = x_ref[...]
    hn = _rms(x, g_ref[...]).astype(BF16)
    z = jnp.dot(hn, wuv_ref[...], preferred_element_type=F32)
    z = 0.5 * z * (1.0 + lax.erf(z * math.sqrt(0.5)))
    u = z[:, :SGU_WIDTH]
    vn = _rms(z[:, SGU_WIDTH:], gv_ref[...]).astype(BF16)
    gd = SGU_WIDTH // SGU_GROUPS
    for r in range(x.shape[0] // SGU_CHUNK):
        rows = slice(r * SGU_CHUNK, (r + 1) * SGU_CHUNK)
        for g in range(SGU_GROUPS):
            cols = slice(g * gd, (g + 1) * gd)
            sv = jnp.dot(ws_ref[g], vn[rows, cols], preferred_element_type=F32) + bs_ref[g]
            y_ref[rows, cols] = (u[rows, cols] * sv).astype(BF16)
    o_ref[...] = x + jnp.dot(y_ref[...], wo_ref[...], preferred_element_type=F32)


def _odd_mixer(x, g, wuv, gv, ws, bs, wo):
    m = x.shape[0]
    tm = ROW_TILE
    gd = SGU_WIDTH // SGU_GROUPS
    return pl.pallas_call(
        _odd_kernel,
        grid=(m // tm,),
        in_specs=[
            pl.BlockSpec((tm, D_MODEL), lambda i: (i, 0)),
            _const_spec((1, D_MODEL)),
            _const_spec((D_MODEL, 2 * SGU_WIDTH)),
            _const_spec((1, SGU_WIDTH)),
            _const_spec((SGU_GROUPS, SGU_CHUNK, SGU_CHUNK)),
            _const_spec((SGU_GROUPS, SGU_CHUNK, gd)),
            _const_spec((SGU_WIDTH, D_MODEL)),
        ],
        out_specs=pl.BlockSpec((tm, D_MODEL), lambda i: (i, 0)),
        out_shape=jax.ShapeDtypeStruct((m, D_MODEL), F32),
        scratch_shapes=[pltpu.VMEM((tm, SGU_WIDTH), BF16)],
        compiler_params=pltpu.CompilerParams(
            dimension_semantics=("arbitrary",), vmem_limit_bytes=VMEM_LIMIT),
        name="sgu_mixer",
    )(x, g, wuv, gv, ws, bs, wo)


def _chunk_cols(w):
    k, n = w.shape
    return jnp.transpose(w.reshape(k, n // FF_CHUNK, FF_CHUNK), (1, 0, 2))


def kernel(x, rel_bias_table, norm_mix_g, norm_ffn_g, even_w_in, even_w_out, diff_lambda,
           diff_subln_g, odd_w_uv, odd_v_norm_g, odd_w_s, odd_b_s, odd_w_out, ffn_w1, ffn_w3,
           ffn_w2, final_norm_g):
    batch, seq, d = x.shape
    assert (seq, d) == (SEQ_LEN, D_MODEL)
    depth = norm_mix_g.shape[0]
    m = batch * seq
    h = x.reshape(m, d)
    a1, a2, twr, twi, cdft = _dft_constants()

    for i in range(depth):
        jx = i // 2
        g_mix = norm_mix_g[i].reshape(1, d)
        if i % 2 == 0:
            lam_init = 0.8 - 0.6 * math.exp(-0.3 * i)
            w_in = even_w_in[jx].astype(BF16)
            n_fqk = FOURIER_WIDTH + 2 * DIFF_WIDTH
            xcs, q, k, vt = _even_in(h, g_mix, w_in[:, :n_fqk], w_in[:, n_fqk:].T, cdft,
                                     batch, seq)
            f = _fourier(xcs, a1, a2, twr, twi, batch)
            bias = _bias_tiles(rel_bias_table)
            att = _attention(q, k, vt, bias, diff_lambda[jx],
                             diff_subln_g[jx].reshape(DIFF_V_DIM, 1), lam_init, batch, seq)
            w_out = even_w_out[jx].astype(BF16)
            h = _even_out(h, f, att, w_out[:FOURIER_WIDTH], w_out[FOURIER_WIDTH:])
        else:
            bs = jnp.broadcast_to(odd_b_s[jx][:, :, None],
                                  (SGU_GROUPS, SGU_CHUNK, SGU_WIDTH // SGU_GROUPS))
            h = _odd_mixer(h, g_mix, odd_w_uv[jx].astype(BF16),
                           odd_v_norm_g[jx].reshape(1, SGU_WIDTH),
                           odd_w_s[jx].astype(BF16), bs, odd_w_out[jx].astype(BF16))
        last = i == depth - 1
        h = _ffn(h, norm_ffn_g[i].reshape(1, d),
                 _chunk_cols(ffn_w1[i].astype(BF16)), _chunk_cols(ffn_w3[i].astype(BF16)),
                 ffn_w2[i].astype(BF16).reshape(D_FF // FF_CHUNK, FF_CHUNK, d),
                 final_norm_g.reshape(1, d), last)
    return h.reshape(batch, seq, d)
```

```python
import functools
import math

import numpy as np
import jax
import jax.numpy as jnp
from jax import lax
from jax.experimental import pallas as pl
from jax.experimental.pallas import tpu as pltpu

F32 = jnp.float32
BF16 = jnp.bfloat16

D_MODEL = 1024
FOURIER_GROUP_DIM = 128
FOURIER_WIDTH = 512
DIFF_HEADS = 4
DIFF_QK_DIM = 64
DIFF_V_DIM = 128
DIFF_WIDTH = 512
REL_BUCKETS = 32
REL_MAX_DIST = 128
SGU_CHUNK = 128
SGU_GROUPS = 8
SGU_WIDTH = 1024
D_FF = 2816
RMS_EPS = 1e-6
LOG2E = math.log2(math.e)

DFT_RADIX = 64
SEQ_LEN = DFT_RADIX * DFT_RADIX

ROW_TILE = 512
ATTN_TILE = 512
ATTN_STREAMS = 2
FF_CHUNK = 256
DFT1_GROUP = 8
DFT2_LANES = 8192
VMEM_LIMIT = 56 * 1024 * 1024


def _rms(x, g):
    return x * lax.rsqrt(jnp.mean(x * x, axis=-1, keepdims=True) + RMS_EPS) * g


def _const_spec(shape):
    return pl.BlockSpec(shape, lambda *_: (0,) * len(shape))


def _dft_constants():
    r = DFT_RADIX
    idx = np.arange(r)
    ang = 2.0 * np.pi * np.outer(idx, idx) / r
    c, s = np.cos(ang), np.sin(ang)
    a1 = np.block([[c, -s], [-s, -c]])
    scale = 1.0 / math.sqrt(SEQ_LEN * FOURIER_GROUP_DIM)
    a2 = np.concatenate([c, s], axis=1) * scale
    tang = 2.0 * np.pi * np.outer(idx, idx) / SEQ_LEN
    twr = np.broadcast_to(np.cos(tang)[:, :, None], (r, r, 128))
    twi = np.broadcast_to(np.sin(tang)[:, :, None], (r, r, 128))
    cidx = np.arange(FOURIER_GROUP_DIM)
    cang = 2.0 * np.pi * np.outer(cidx, cidx) / FOURIER_GROUP_DIM
    eye2 = np.eye(2)
    cc = np.kron(eye2, np.cos(cang))
    cs = np.kron(eye2, np.sin(cang))
    return (jnp.asarray(a1, BF16), jnp.asarray(a2, BF16),
            jnp.asarray(twr, F32), jnp.asarray(twi, F32),
            jnp.asarray(np.stack([cc, cs]), BF16))


def _even_in_kernel(x_ref, g_ref, w_ref, wvt_ref, cdft_ref, xcs_ref, q_ref, k_ref, vt_ref):
    hn = _rms(x_ref[...], g_ref[...]).astype(BF16)
    z = jnp.dot(hn, w_ref[...], preferred_element_type=F32)
    vt = lax.dot_general(wvt_ref[...], hn, (((1,), (1,)), ((), ())),
                         preferred_element_type=F32)
    for hh in range(DIFF_HEADS):
        vt_ref[0, hh, 0] = vt[hh * DIFF_V_DIM:(hh + 1) * DIFF_V_DIM].astype(BF16)
    zf = z[:, :FOURIER_WIDTH].astype(BF16)
    for p in range(FOURIER_WIDTH // 256):
        zp = zf[:, p * 256:(p + 1) * 256]
        xcs_ref[0, 0, :, p * 256:(p + 1) * 256] = jnp.dot(
            zp, cdft_ref[0], preferred_element_type=F32).astype(BF16)
        xcs_ref[0, 1, :, p * 256:(p + 1) * 256] = jnp.dot(
            zp, cdft_ref[1], preferred_element_type=F32).astype(BF16)
    o = FOURIER_WIDTH
    q_ref[...] = (z[:, o:o + 512] * (DIFF_QK_DIM ** -0.5 * LOG2E)).astype(BF16)
    k_ref[...] = z[:, o + 512:o + 1024].astype(BF16)


def _even_in(x, g, w, wvt, cdft, batch, seq):
    m = x.shape[0]
    tm = ROW_TILE
    assert tm == ATTN_TILE
    per_b = seq // tm
    n_out = w.shape[1]
    return pl.pallas_call(
        _even_in_kernel,
        grid=(m // tm,),
        in_specs=[
            pl.BlockSpec((tm, D_MODEL), lambda i: (i, 0)),
            _const_spec((1, D_MODEL)),
            _const_spec((D_MODEL, n_out)),
            _const_spec((DIFF_WIDTH, D_MODEL)),
            _const_spec((2, 256, 256)),
        ],
        out_specs=[
            pl.BlockSpec((1, 2, tm, FOURIER_WIDTH), lambda i: (i // per_b, 0, i % per_b, 0)),
            pl.BlockSpec((tm, 512), lambda i: (i, 0)),
            pl.BlockSpec((tm, 512), lambda i: (i, 0)),
            pl.BlockSpec((1, DIFF_HEADS, 1, DIFF_V_DIM, tm),
                         lambda i: (i // per_b, 0, i % per_b, 0, 0)),
        ],
        out_shape=[
            jax.ShapeDtypeStruct((batch, 2, seq, FOURIER_WIDTH), BF16),
            jax.ShapeDtypeStruct((m, 512), BF16),
            jax.ShapeDtypeStruct((m, 512), BF16),
            jax.ShapeDtypeStruct((batch, DIFF_HEADS, per_b, DIFF_V_DIM, tm), BF16),
        ],
        compiler_params=pltpu.CompilerParams(
            dimension_semantics=("arbitrary",), vmem_limit_bytes=VMEM_LIMIT),
        name="even_in_proj",
    )(x, g, w, wvt, cdft)


def _dft1_kernel(x_ref, a1_ref, twr_ref, twi_ref, y_ref):
    r = DFT_RADIX
    y = jnp.dot(a1_ref[...], x_ref[0], preferred_element_type=F32)
    for i in range(DFT1_GROUP):
        c = twr_ref[i]
        s = twi_ref[i]
        for jj in range(FOURIER_WIDTH // 128):
            lo = i * FOURIER_WIDTH + jj * 128
            yr = y[:r, lo:lo + 128]
            yi = y[r:, lo:lo + 128]
            y_ref[0, 0, i, :, jj * 128:(jj + 1) * 128] = (yr * c + yi * s).astype(BF16)
            y_ref[0, 1, i, :, jj * 128:(jj + 1) * 128] = (yi * c - yr * s).astype(BF16)


def _dft2_kernel(y_ref, a2_ref, f_ref):
    f_ref[0] = jnp.dot(a2_ref[...], y_ref[0], preferred_element_type=F32).astype(BF16)


def _fourier(xcs, a1, a2, twr, twi, batch):
    r = DFT_RADIX
    lanes = r * FOURIER_WIDTH
    g = DFT1_GROUP
    x2 = xcs.reshape(batch, 2 * r, lanes)
    yt = pl.pallas_call(
        _dft1_kernel,
        grid=(batch, r // g),
        in_specs=[
            pl.BlockSpec((1, 2 * r, g * FOURIER_WIDTH), lambda b, j: (b, 0, j)),
            _const_spec((2 * r, 2 * r)),
            pl.BlockSpec((g, r, 128), lambda b, j: (j, 0, 0)),
            pl.BlockSpec((g, r, 128), lambda b, j: (j, 0, 0)),
        ],
        out_specs=pl.BlockSpec((1, 2, g, r, FOURIER_WIDTH), lambda b, j: (b, 0, j, 0, 0)),
        out_shape=jax.ShapeDtypeStruct((batch, 2, r, r, FOURIER_WIDTH), BF16),
        compiler_params=pltpu.CompilerParams(
            dimension_semantics=("arbitrary", "arbitrary"), vmem_limit_bytes=VMEM_LIMIT),
        name="seq_dft_stage1",
    )(x2, a1, twr, twi)
    y2 = yt.reshape(batch, 2 * r, lanes)
    f = pl.pallas_call(
        _dft2_kernel,
        grid=(batch, lanes // DFT2_LANES),
        in_specs=[
            pl.BlockSpec((1, 2 * r, DFT2_LANES), lambda b, j: (b, 0, j)),
            _const_spec((r, 2 * r)),
        ],
        out_specs=pl.BlockSpec((1, r, DFT2_LANES), lambda b, j: (b, 0, j)),
        out_shape=jax.ShapeDtypeStruct((batch, r, lanes), BF16),
        compiler_params=pltpu.CompilerParams(
            dimension_semantics=("arbitrary", "arbitrary"), vmem_limit_bytes=VMEM_LIMIT),
        name="seq_dft_stage2",
    )(y2, a2)
    return f.reshape(batch * SEQ_LEN, FOURIER_WIDTH)


def _bias_tiles_kernel(tab_ref, o_ref):
    h = pl.program_id(0)
    tt = ATTN_TILE
    half = REL_BUCKETS // 2
    max_exact = half // 2
    row = lax.broadcasted_iota(jnp.int32, (tt, tt), 0)
    col = lax.broadcasted_iota(jnp.int32, (tt, tt), 1)
    for t in range(5):
        rel = row - col + (t - 2) * tt
        ret = jnp.where(rel > 0, half, 0)
        n = jnp.abs(rel)
        nf = jnp.maximum(n, 1).astype(F32)
        large = max_exact + (jnp.log(nf / max_exact) / math.log(REL_MAX_DIST / max_exact)
                             * (half - max_exact)).astype(jnp.int32)
        large = jnp.minimum(large, half - 1)
        bucket = ret + jnp.where(n < max_exact, n, large)
        lo = half if t > 2 else 0
        hi = half if t < 2 else REL_BUCKETS
        acc = jnp.zeros((tt, tt), F32)
        for bk in range(lo, hi):
            acc = jnp.where(bucket == bk, tab_ref[bk, h], acc)
        o_ref[0, t] = acc * LOG2E


def _bias_tiles(table):
    tt = ATTN_TILE
    return pl.pallas_call(
        _bias_tiles_kernel,
        grid=(DIFF_HEADS,),
        in_specs=[pl.BlockSpec(memory_space=pltpu.SMEM)],
        out_specs=pl.BlockSpec((1, 5, tt, tt), lambda h: (h, 0, 0, 0)),
        out_shape=jax.ShapeDtypeStruct((DIFF_HEADS, 5, tt, tt), F32),
        compiler_params=pltpu.CompilerParams(
            dimension_semantics=("arbitrary",), vmem_limit_bytes=VMEM_LIMIT),
        name="rel_bias_tiles",
    )(table)


def _attn_kernel(q_ref, k_ref, vt_ref, bias_ref, lam_ref, g_ref, o_ref, *, lam_init, n_kv):
    i = pl.program_id(2)
    tt = ATTN_TILE
    q = q_ref[...].astype(F32)
    lane = lax.broadcasted_iota(jnp.int32, q.shape, 1)
    qm = [jnp.where(lane < DIFF_QK_DIM, q, 0.0).astype(BF16),
          jnp.where(lane >= DIFF_QK_DIM, q, 0.0).astype(BF16)]
    ns = ATTN_STREAMS
    m = [[None] * ns for _ in range(2)]
    l = [[None] * ns for _ in range(2)]
    acc = [[None] * ns for _ in range(2)]
    def scores(j, mp):
        bias = bias_ref[0, jnp.clip(j - i, -2, 2) + 2]
        kt = k_ref[j * tt:(j + 1) * tt, :]
        return lax.dot_general(kt, qm[mp], (((1,), (1,)), ((), ())),
                               preferred_element_type=F32) + bias

    units = [(j, mp) for j in range(n_kv) for mp in range(2)]
    s_next = scores(*units[0])
    for u, (j, mp) in enumerate(units):
        s = s_next
        if u + 1 < len(units):
            s_next = scores(*units[u + 1])
        st = j % ns
        vt = vt_ref[0, 0, j]
        smax = jnp.max(s, axis=0, keepdims=True)
        if m[mp][st] is None:
            p = jnp.exp2(s - smax)
            l[mp][st] = jnp.sum(p, axis=0, keepdims=True)
            acc[mp][st] = jnp.dot(vt, p.astype(BF16), preferred_element_type=F32)
            m[mp][st] = smax
        else:
            m_new = jnp.maximum(m[mp][st], smax)
            alpha = jnp.exp2(m[mp][st] - m_new)
            p = jnp.exp2(s - m_new)
            l[mp][st] = alpha * l[mp][st] + jnp.sum(p, axis=0, keepdims=True)
            acc[mp][st] = alpha * acc[mp][st] + jnp.dot(
                vt, p.astype(BF16), preferred_element_type=F32)
            m[mp][st] = m_new
    lm, accm = [], []
    for mp in range(2):
        mm = functools.reduce(jnp.maximum, m[mp])
        w = [jnp.exp2(ms - mm) for ms in m[mp]]
        lm.append(sum(ws * ls for ws, ls in zip(w, l[mp])))
        accm.append(sum(ws * a for ws, a in zip(w, acc[mp])))
    lf = lam_ref[...]
    lam = (jnp.exp(jnp.sum(lf[0:1] * lf[1:2], axis=-1, keepdims=True))
           - jnp.exp(jnp.sum(lf[2:3] * lf[3:4], axis=-1, keepdims=True)) + lam_init)
    o = accm[0] / lm[0] - lam * (accm[1] / lm[1])
    y = (o * lax.rsqrt(jnp.mean(o * o, axis=0, keepdims=True) + RMS_EPS) * g_ref[...]
         * (1.0 - lam_init))
    o_ref[...] = y.T.astype(BF16)


def _attention(q, k, vt, bias, lam, subln_g, lam_init, batch, seq):
    tt = ATTN_TILE
    nq = seq // tt
    m = batch * seq
    kern = functools.partial(_attn_kernel, lam_init=lam_init, n_kv=nq)
    return pl.pallas_call(
        kern,
        grid=(batch, DIFF_HEADS, nq),
        in_specs=[
            pl.BlockSpec((tt, DIFF_V_DIM), lambda b, h, i: (b * nq + i, h)),
            pl.BlockSpec((seq, DIFF_V_DIM), lambda b, h, i: (b, h)),
            pl.BlockSpec((1, 1, nq, DIFF_V_DIM, tt), lambda b, h, i: (b, h, 0, 0, 0)),
            pl.BlockSpec((1, 5, tt, tt), lambda b, h, i: (h, 0, 0, 0)),
            _const_spec((4, DIFF_QK_DIM)),
            _const_spec((DIFF_V_DIM, 1)),
        ],
        out_specs=pl.BlockSpec((tt, DIFF_V_DIM), lambda b, h, i: (b * nq + i, h)),
        out_shape=jax.ShapeDtypeStruct((m, DIFF_WIDTH), BF16),
        compiler_params=pltpu.CompilerParams(
            dimension_semantics=("arbitrary", "arbitrary", "arbitrary"),
            vmem_limit_bytes=VMEM_LIMIT),
        name="diff_attention",
    )(q, k, vt, bias, lam, subln_g)


def _even_out_kernel(x_ref, f_ref, a_ref, wf_ref, wa_ref, o_ref):
    o_ref[...] = (x_ref[...]
                  + jnp.dot(f_ref[...], wf_ref[...], preferred_element_type=F32)
                  + jnp.dot(a_ref[...], wa_ref[...], preferred_element_type=F32))


def _even_out(x, f, a, wf, wa):
    m = x.shape[0]
    tm = ROW_TILE
    return pl.pallas_call(
        _even_out_kernel,
        grid=(m // tm,),
        in_specs=[
            pl.BlockSpec((tm, D_MODEL), lambda i: (i, 0)),
            pl.BlockSpec((tm, FOURIER_WIDTH), lambda i: (i, 0)),
            pl.BlockSpec((tm, DIFF_WIDTH), lambda i: (i, 0)),
            _const_spec((FOURIER_WIDTH, D_MODEL)),
            _const_spec((DIFF_WIDTH, D_MODEL)),
        ],
        out_specs=pl.BlockSpec((tm, D_MODEL), lambda i: (i, 0)),
        out_shape=jax.ShapeDtypeStruct((m, D_MODEL), F32),
        compiler_params=pltpu.CompilerParams(
            dimension_semantics=("arbitrary",), vmem_limit_bytes=VMEM_LIMIT),
        name="even_out_proj",
    )(x, f, a, wf, wa)


def _ffn_kernel(x_ref, g_ref, w1_ref, w3_ref, w2_ref, gf_ref, o_ref, *, final_norm):
    x = x_ref[...]
    hn = _rms(x, g_ref[...]).astype(BF16)
    acc = x
    for c in range(D_FF // FF_CHUNK):
        a = jnp.dot(hn, w1_ref[c], preferred_element_type=F32)
        b = jnp.dot(hn, w3_ref[c], preferred_element_type=F32)
        u = (a * jax.nn.sigmoid(a) * b).astype(BF16)
        acc = acc + jnp.dot(u, w2_ref[c], preferred_element_type=F32)
    if final_norm:
        acc = _rms(acc, gf_ref[...])
    o_ref[...] = acc


def _ffn(x, g, w1, w3, w2, gf, final_norm):
    m = x.shape[0]
    tm = ROW_TILE
    nc = D_FF // FF_CHUNK
    return pl.pallas_call(
        functools.partial(_ffn_kernel, final_norm=final_norm),
        grid=(m // tm,),
        in_specs=[
            pl.BlockSpec((tm, D_MODEL), lambda i: (i, 0)),
            _const_spec((1, D_MODEL)),
            _const_spec((nc, D_MODEL, FF_CHUNK)),
            _const_spec((nc, D_MODEL, FF_CHUNK)),
            _const_spec((nc, FF_CHUNK, D_MODEL)),
            _const_spec((1, D_MODEL)),
        ],
        out_specs=pl.BlockSpec((tm, D_MODEL), lambda i: (i, 0)),
        out_shape=jax.ShapeDtypeStruct((m, D_MODEL), F32),
        compiler_params=pltpu.CompilerParams(
            dimension_semantics=("arbitrary",), vmem_limit_bytes=VMEM_LIMIT),
        name="swiglu_ffn",
    )(x, g, w1, w3, w2, gf)


def _odd_kernel(x_ref, g_ref, wuv_ref, gv_ref, ws_ref, bs_ref, wo_ref, o_ref, y_ref):
    x = x_ref[...]
    hn = _rms(x, g_ref[...]).astype(BF16)
    z = jnp.dot(hn, wuv_ref[...], preferred_element_type=F32)
    z = 0.5 * z * (1.0 + lax.erf(z * math.sqrt(0.5)))
    u = z[:, :SGU_WIDTH]
    vn = _rms(z[:, SGU_WIDTH:], gv_ref[...]).astype(BF16)
    gd = SGU_WIDTH // SGU_GROUPS
    for r in range(x.shape[0] // SGU_CHUNK):
        rows = slice(r * SGU_CHUNK, (r + 1) * SGU_CHUNK)
        for g in range(SGU_GROUPS):
            cols = slice(g * gd, (g + 1) * gd)
            sv = jnp.dot(ws_ref[g], vn[rows, cols], preferred_element_type=F32) + bs_ref[g]
            y_ref[rows, cols] = (u[rows, cols] * sv).astype(BF16)
    o_ref[...] = x + jnp.dot(y_ref[...], wo_ref[...], preferred_element_type=F32)


def _odd_mixer(x, g, wuv, gv, ws, bs, wo):
    m = x.shape[0]
    tm = ROW_TILE
    gd = SGU_WIDTH // SGU_GROUPS
    return pl.pallas_call(
        _odd_kernel,
        grid=(m // tm,),
        in_specs=[
            pl.BlockSpec((tm, D_MODEL), lambda i: (i, 0)),
            _const_spec((1, D_MODEL)),
            _const_spec((D_MODEL, 2 * SGU_WIDTH)),
            _const_spec((1, SGU_WIDTH)),
            _const_spec((SGU_GROUPS, SGU_CHUNK, SGU_CHUNK)),
            _const_spec((SGU_GROUPS, SGU_CHUNK, gd)),
            _const_spec((SGU_WIDTH, D_MODEL)),
        ],
        out_specs=pl.BlockSpec((tm, D_MODEL), lambda i: (i, 0)),
        out_shape=jax.ShapeDtypeStruct((m, D_MODEL), F32),
        scratch_shapes=[pltpu.VMEM((tm, SGU_WIDTH), BF16)],
        compiler_params=pltpu.CompilerParams(
            dimension_semantics=("arbitrary",), vmem_limit_bytes=VMEM_LIMIT),
        name="sgu_mixer",
    )(x, g, wuv, gv, ws, bs, wo)


def _chunk_cols(w):
    k, n = w.shape
    return jnp.transpose(w.reshape(k, n // FF_CHUNK, FF_CHUNK), (1, 0, 2))


def kernel(x, rel_bias_table, norm_mix_g, norm_ffn_g, even_w_in, even_w_out, diff_lambda,
           diff_subln_g, odd_w_uv, odd_v_norm_g, odd_w_s, odd_b_s, odd_w_out, ffn_w1, ffn_w3,
           ffn_w2, final_norm_g):
    batch, seq, d = x.shape
    assert (seq, d) == (SEQ_LEN, D_MODEL)
    depth = norm_mix_g.shape[0]
    m = batch * seq
    h = x.reshape(m, d)
    a1, a2, twr, twi, cdft = _dft_constants()

    for i in range(depth):
        jx = i // 2
        g_mix = norm_mix_g[i].reshape(1, d)
        if i % 2 == 0:
            lam_init = 0.8 - 0.6 * math.exp(-0.3 * i)
            w_in = even_w_in[jx].astype(BF16)
            n_fqk = FOURIER_WIDTH + 2 * DIFF_WIDTH
            xcs, q, k, vt = _even_in(h, g_mix, w_in[:, :n_fqk], w_in[:, n_fqk:].T, cdft,
                                     batch, seq)
            f = _fourier(xcs, a1, a2, twr, twi, batch)
            bias = _bias_tiles(rel_bias_table)
            att = _attention(q, k, vt, bias, diff_lambda[jx],
                             diff_subln_g[jx].reshape(DIFF_V_DIM, 1), lam_init, batch, seq)
            w_out = even_w_out[jx].astype(BF16)
            h = _even_out(h, f, att, w_out[:FOURIER_WIDTH], w_out[FOURIER_WIDTH:])
        else:
            bs = jnp.broadcast_to(odd_b_s[jx][:, :, None],
                                  (SGU_GROUPS, SGU_CHUNK, SGU_WIDTH // SGU_GROUPS))
            h = _odd_mixer(h, g_mix, odd_w_uv[jx].astype(BF16),
                           odd_v_norm_g[jx].reshape(1, SGU_WIDTH),
                           odd_w_s[jx].astype(BF16), bs, odd_w_out[jx].astype(BF16))
        last = i == depth - 1
        h = _ffn(h, norm_ffn_g[i].reshape(1, d),
                 _chunk_cols(ffn_w1[i].astype(BF16)), _chunk_cols(ffn_w3[i].astype(BF16)),
                 ffn_w2[i].astype(BF16).reshape(D_FF // FF_CHUNK, FF_CHUNK, d),
                 final_norm_g.reshape(1, d), last)
    return h.reshape(batch, seq, d)
```

```python
import functools
import math

import numpy as np
import jax
import jax.numpy as jnp
from jax import lax
from jax.experimental import pallas as pl
from jax.experimental.pallas import tpu as pltpu

F32 = jnp.float32
BF16 = jnp.bfloat16

D_MODEL = 1024
FOURIER_GROUP_DIM = 128
FOURIER_WIDTH = 512
DIFF_HEADS = 4
DIFF_QK_DIM = 64
DIFF_V_DIM = 128
DIFF_WIDTH = 512
REL_BUCKETS = 32
REL_MAX_DIST = 128
SGU_CHUNK = 128
SGU_GROUPS = 8
SGU_WIDTH = 1024
D_FF = 2816
RMS_EPS = 1e-6
LOG2E = math.log2(math.e)

DFT_RADIX = 16
SEQ_LEN = DFT_RADIX ** 3

ROW_TILE = 512
ATTN_TILE = 512
ATTN_STREAMS = 2
FF_CHUNK = 256
VMEM_LIMIT = 56 * 1024 * 1024


def _rms(x, g):
    return x * lax.rsqrt(jnp.mean(x * x, axis=-1, keepdims=True) + RMS_EPS) * g


def _const_spec(shape):
    return pl.BlockSpec(shape, lambda *_: (0,) * len(shape))


def _dft_constants():
    r = DFT_RADIX
    n = SEQ_LEN
    i = np.arange(r)
    f = np.exp(-2j * np.pi * np.outer(i, i) / r)
    eye = np.eye(r)

    def real_rep(g):
        return np.block([[g.real, -g.imag], [g.imag, g.real]])

    stage_a, stage_b = [], []
    for t in range(r):
        tw = np.exp(-2j * np.pi * t * i / (r * r))
        g = np.einsum('pr,kn,k->pknr', eye, f, tw).reshape(r * r, r * r)
        stage_a.append(real_rep(g))
    for rr in range(r):
        tw = np.exp(-2j * np.pi * rr * (i[None, :] + r * i[:, None]) / n)
        g = np.einsum('qk,jt,jq->jqtk', eye, f, tw).reshape(r * r, r * r)
        stage_b.append(real_rep(g))
    scale = 1.0 / math.sqrt(n * FOURIER_GROUP_DIM)
    g = np.einsum('qk,jr->jqrk', eye, f).reshape(r * r, r * r) * scale
    stage_c = np.concatenate([g.real, -g.imag], axis=1)
    cidx = np.arange(FOURIER_GROUP_DIM)
    cang = 2.0 * np.pi * np.outer(cidx, cidx) / FOURIER_GROUP_DIM
    eye2 = np.eye(2)
    cdft = np.stack([np.kron(eye2, np.cos(cang)), -np.kron(eye2, np.sin(cang))])
    return (jnp.asarray(np.stack(stage_a), BF16), jnp.asarray(np.stack(stage_b), BF16),
            jnp.asarray(stage_c, BF16), jnp.asarray(cdft, BF16))


def _even_in_kernel(x_ref, g_ref, w_ref, wvt_ref, cdft_ref, xcs_ref, q_ref, k_ref, vt_ref):
    hn = _rms(x_ref[...], g_ref[...]).astype(BF16)
    z = jnp.dot(hn, w_ref[...], preferred_element_type=F32)
    vt = lax.dot_general(wvt_ref[...], hn, (((1,), (1,)), ((), ())),
                         preferred_element_type=F32)
    for hh in range(DIFF_HEADS):
        vt_ref[0, hh, 0] = vt[hh * DIFF_V_DIM:(hh + 1) * DIFF_V_DIM].astype(BF16)
    zf = z[:, :FOURIER_WIDTH].astype(BF16)
    for p in range(FOURIER_WIDTH // 256):
        zp = zf[:, p * 256:(p + 1) * 256]
        xcs_ref[0, 0, :, p * 256:(p + 1) * 256] = jnp.dot(
            zp, cdft_ref[0], preferred_element_type=F32).astype(BF16)
        xcs_ref[0, 1, :, p * 256:(p + 1) * 256] = jnp.dot(
            zp, cdft_ref[1], preferred_element_type=F32).astype(BF16)
    o = FOURIER_WIDTH
    q_ref[...] = (z[:, o:o + 512] * (DIFF_QK_DIM ** -0.5 * LOG2E)).astype(BF16)
    k_ref[...] = z[:, o + 512:o + 1024].astype(BF16)


def _even_in(x, g, w, wvt, cdft, batch, seq):
    m = x.shape[0]
    tm = ROW_TILE
    assert tm == ATTN_TILE
    per_b = seq // tm
    n_out = w.shape[1]
    return pl.pallas_call(
        _even_in_kernel,
        grid=(m // tm,),
        in_specs=[
            pl.BlockSpec((tm, D_MODEL), lambda i: (i, 0)),
            _const_spec((1, D_MODEL)),
            _const_spec((D_MODEL, n_out)),
            _const_spec((DIFF_WIDTH, D_MODEL)),
            _const_spec((2, 256, 256)),
        ],
        out_specs=[
            pl.BlockSpec((1, 2, tm, FOURIER_WIDTH), lambda i: (i // per_b, 0, i % per_b, 0)),
            pl.BlockSpec((tm, 512), lambda i: (i, 0)),
            pl.BlockSpec((tm, 512), lambda i: (i, 0)),
            pl.BlockSpec((1, DIFF_HEADS, 1, DIFF_V_DIM, tm),
                         lambda i: (i // per_b, 0, i % per_b, 0, 0)),
        ],
        out_shape=[
            jax.ShapeDtypeStruct((batch, 2, seq, FOURIER_WIDTH), BF16),
            jax.ShapeDtypeStruct((m, 512), BF16),
            jax.ShapeDtypeStruct((m, 512), BF16),
            jax.ShapeDtypeStruct((batch, DIFF_HEADS, per_b, DIFF_V_DIM, tm), BF16),
        ],
        compiler_params=pltpu.CompilerParams(
            dimension_semantics=("arbitrary",), vmem_limit_bytes=VMEM_LIMIT),
        name="even_in_proj",
    )(x, g, w, wvt, cdft)


def _dft_stage_kernel(x_ref, w_ref, o_ref):
    rows = x_ref.shape[0] * x_ref.shape[1] * x_ref.shape[2]
    w = w_ref[0] if len(w_ref.shape) == 3 else w_ref[...]
    y = jnp.dot(w, x_ref[...].reshape(rows, FOURIER_WIDTH), preferred_element_type=F32)
    o_ref[...] = y.astype(BF16).reshape(o_ref.shape)


def _fourier(xcs, stage_a, stage_b, stage_c, batch):
    r = DFT_RADIX
    c = FOURIER_WIDTH
    nn = 2 * r * r
    params = pltpu.CompilerParams(
        dimension_semantics=("arbitrary", "arbitrary"), vmem_limit_bytes=VMEM_LIMIT)
    fix_mid = pl.BlockSpec((None, 2, r, None, r, c), lambda d, b: (b, 0, 0, d, 0, 0))
    fix_out = pl.BlockSpec((None, 2, None, r, r, c), lambda d, b: (b, 0, d, 0, 0, 0))
    per_digit = pl.BlockSpec((1, nn, nn), lambda d, b: (d, 0, 0))
    planes = jax.ShapeDtypeStruct((batch, 2, r, r, r, c), BF16)
    x6 = xcs.reshape(batch, 2, r, r, r, c)
    y = pl.pallas_call(
        _dft_stage_kernel, grid=(r, batch), in_specs=[fix_mid, per_digit],
        out_specs=fix_mid, out_shape=planes, compiler_params=params,
        name="seq_dft_stage_a")(x6, stage_a)
    z = pl.pallas_call(
        _dft_stage_kernel, grid=(r, batch), in_specs=[fix_out, per_digit],
        out_specs=fix_out, out_shape=planes, compiler_params=params,
        name="seq_dft_stage_b")(y, stage_b)
    f = pl.pallas_call(
        _dft_stage_kernel, grid=(r, batch),
        in_specs=[fix_mid, _const_spec((r * r, nn))],
        out_specs=pl.BlockSpec((None, r, None, r, c), lambda d, b: (b, 0, d, 0, 0)),
        out_shape=jax.ShapeDtypeStruct((batch, r, r, r, c), BF16), compiler_params=params,
        name="seq_dft_stage_c")(z, stage_c)
    return f.reshape(batch * SEQ_LEN, c)


def _bias_tiles_kernel(tab_ref, o_ref):
    h = pl.program_id(0)
    tt = ATTN_TILE
    half = REL_BUCKETS // 2
    max_exact = half // 2
    row = lax.broadcasted_iota(jnp.int32, (tt, tt), 0)
    col = lax.broadcasted_iota(jnp.int32, (tt, tt), 1)
    for t in range(5):
        rel = row - col + (t - 2) * tt
        ret = jnp.where(rel > 0, half, 0)
        n = jnp.abs(rel)
        nf = jnp.maximum(n, 1).astype(F32)
        large = max_exact + (jnp.log(nf / max_exact) / math.log(REL_MAX_DIST / max_exact)
                             * (half - max_exact)).astype(jnp.int32)
        large = jnp.minimum(large, half - 1)
        bucket = ret + jnp.where(n < max_exact, n, large)
        lo = half if t > 2 else 0
        hi = half if t < 2 else REL_BUCKETS
        acc = jnp.zeros((tt, tt), F32)
        for bk in range(lo, hi):
            acc = jnp.where(bucket == bk, tab_ref[bk, h], acc)
        o_ref[0, t] = acc * LOG2E


def _bias_tiles(table):
    tt = ATTN_TILE
    return pl.pallas_call(
        _bias_tiles_kernel,
        grid=(DIFF_HEADS,),
        in_specs=[pl.BlockSpec(memory_space=pltpu.SMEM)],
        out_specs=pl.BlockSpec((1, 5, tt, tt), lambda h: (h, 0, 0, 0)),
        out_shape=jax.ShapeDtypeStruct((DIFF_HEADS, 5, tt, tt), F32),
        compiler_params=pltpu.CompilerParams(
            dimension_semantics=("arbitrary",), vmem_limit_bytes=VMEM_LIMIT),
        name="rel_bias_tiles",
    )(table)


def _attn_kernel(q_ref, k_ref, vt_ref, bias_ref, lam_ref, g_ref, o_ref, *, lam_init, n_kv):
    i = pl.program_id(2)
    tt = ATTN_TILE
    q = q_ref[...].astype(F32)
    lane = lax.broadcasted_iota(jnp.int32, q.shape, 1)
    qm = [jnp.where(lane < DIFF_QK_DIM, q, 0.0).astype(BF16),
          jnp.where(lane >= DIFF_QK_DIM, q, 0.0).astype(BF16)]
    ns = ATTN_STREAMS
    m = [[None] * ns for _ in range(2)]
    l = [[None] * ns for _ in range(2)]
    acc = [[None] * ns for _ in range(2)]
    def scores(j, mp):
        bias = bias_ref[0, jnp.clip(j - i, -2, 2) + 2]
        kt = k_ref[j * tt:(j + 1) * tt, :]
        return lax.dot_general(kt, qm[mp], (((1,), (1,)), ((), ())),
                               preferred_element_type=F32) + bias

    units = [(j, mp) for j in range(n_kv) for mp in range(2)]
    s_next = scores(*units[0])
    for u, (j, mp) in enumerate(units):
        s = s_next
        if u + 1 < len(units):
            s_next = scores(*units[u + 1])
        st = j % ns
        vt = vt_ref[0, 0, j]
        smax = jnp.max(s, axis=0, keepdims=True)
        if m[mp][st] is None:
            p = jnp.exp2(s - smax)
            l[mp][st] = jnp.sum(p, axis=0, keepdims=True)
            acc[mp][st] = jnp.dot(vt, p.astype(BF16), preferred_element_type=F32)
            m[mp][st] = smax
        else:
            m_new = jnp.maximum(m[mp][st], smax)
            alpha = jnp.exp2(m[mp][st] - m_new)
            p = jnp.exp2(s - m_new)
            l[mp][st] = alpha * l[mp][st] + jnp.sum(p, axis=0, keepdims=True)
            acc[mp][st] = alpha * acc[mp][st] + jnp.dot(
                vt, p.astype(BF16), preferred_element_type=F32)
            m[mp][st] = m_new
    lm, accm = [], []
    for mp in range(2):
        mm = functools.reduce(jnp.maximum, m[mp])
        w = [jnp.exp2(ms - mm) for ms in m[mp]]
        lm.append(sum(ws * ls for ws, ls in zip(w, l[mp])))
        accm.append(sum(ws * a for ws, a in zip(w, acc[mp])))
    lf = lam_ref[...]
    lam = (jnp.exp(jnp.sum(lf[0:1] * lf[1:2], axis=-1, keepdims=True))
           - jnp.exp(jnp.sum(lf[2:3] * lf[3:4], axis=-1, keepdims=True)) + lam_init)
    o = accm[0] / lm[0] - lam * (accm[1] / lm[1])
    y = (o * lax.rsqrt(jnp.mean(o * o, axis=0, keepdims=True) + RMS_EPS) * g_ref[...]
         * (1.0 - lam_init))
    o_ref[...] = y.T.astype(BF16)


def _attention(q, k, vt, bias, lam, subln_g, lam_init, batch, seq):
    tt = ATTN_TILE
    nq = seq // tt
    m = batch * seq
    kern = functools.partial(_attn_kernel, lam_init=lam_init, n_kv=nq)
    return pl.pallas_call(
        kern,
        grid=(batch, DIFF_HEADS, nq),
        in_specs=[
            pl.BlockSpec((tt, DIFF_V_DIM), lambda b, h, i: (b * nq + i, h)),
            pl.BlockSpec((seq, DIFF_V_DIM), lambda b, h, i: (b, h)),
            pl.BlockSpec((1, 1, nq, DIFF_V_DIM, tt), lambda b, h, i: (b, h, 0, 0, 0)),
            pl.BlockSpec((1, 5, tt, tt), lambda b, h, i: (h, 0, 0, 0)),
            _const_spec((4, DIFF_QK_DIM)),
            _const_spec((DIFF_V_DIM, 1)),
        ],
        out_specs=pl.BlockSpec((tt, DIFF_V_DIM), lambda b, h, i: (b * nq + i, h)),
        out_shape=jax.ShapeDtypeStruct((m, DIFF_WIDTH), BF16),
        compiler_params=pltpu.CompilerParams(
            dimension_semantics=("arbitrary", "arbitrary", "arbitrary"),
            vmem_limit_bytes=VMEM_LIMIT),
        name="diff_attention",
    )(q, k, vt, bias, lam, subln_g)


def _even_out_kernel(x_ref, f_ref, a_ref, wf_ref, wa_ref, o_ref):
    o_ref[...] = (x_ref[...]
                  + jnp.dot(f_ref[...], wf_ref[...], preferred_element_type=F32)
                  + jnp.dot(a_ref[...], wa_ref[...], preferred_element_type=F32))


def _even_out(x, f, a, wf, wa):
    m = x.shape[0]
    tm = ROW_TILE
    return pl.pallas_call(
        _even_out_kernel,
        grid=(m // tm,),
        in_specs=[
            pl.BlockSpec((tm, D_MODEL), lambda i: (i, 0)),
            pl.BlockSpec((tm, FOURIER_WIDTH), lambda i: (i, 0)),
            pl.BlockSpec((tm, DIFF_WIDTH), lambda i: (i, 0)),
            _const_spec((FOURIER_WIDTH, D_MODEL)),
            _const_spec((DIFF_WIDTH, D_MODEL)),
        ],
        out_specs=pl.BlockSpec((tm, D_MODEL), lambda i: (i, 0)),
        out_shape=jax.ShapeDtypeStruct((m, D_MODEL), F32),
        compiler_params=pltpu.CompilerParams(
            dimension_semantics=("arbitrary",), vmem_limit_bytes=VMEM_LIMIT),
        name="even_out_proj",
    )(x, f, a, wf, wa)


def _ffn_kernel(x_ref, g_ref, w1_ref, w3_ref, w2_ref, gf_ref, o_ref, *, final_norm):
    x = x_ref[...]
    hn = _rms(x, g_ref[...]).astype(BF16)
    acc = x
    for c in range(D_FF // FF_CHUNK):
        a = jnp.dot(hn, w1_ref[c], preferred_element_type=F32)
        b = jnp.dot(hn, w3_ref[c], preferred_element_type=F32)
        u = (a * jax.nn.sigmoid(a) * b).astype(BF16)
        acc = acc + jnp.dot(u, w2_ref[c], preferred_element_type=F32)
    if final_norm:
        acc = _rms(acc, gf_ref[...])
    o_ref[...] = acc


def _ffn(x, g, w1, w3, w2, gf, final_norm):
    m = x.shape[0]
    tm = ROW_TILE
    nc = D_FF // FF_CHUNK
    return pl.pallas_call(
        functools.partial(_ffn_kernel, final_norm=final_norm),
        grid=(m // tm,),
        in_specs=[
            pl.BlockSpec((tm, D_MODEL), lambda i: (i, 0)),
            _const_spec((1, D_MODEL)),
            _const_spec((nc, D_MODEL, FF_CHUNK)),
            _const_spec((nc, D_MODEL, FF_CHUNK)),
            _const_spec((nc, FF_CHUNK, D_MODEL)),
            _const_spec((1, D_MODEL)),
        ],
        out_specs=pl.BlockSpec((tm, D_MODEL), lambda i: (i, 0)),
        out_shape=jax.ShapeDtypeStruct((m, D_MODEL), F32),
        compiler_params=pltpu.CompilerParams(
            dimension_semantics=("arbitrary",), vmem_limit_bytes=VMEM_LIMIT),
        name="swiglu_ffn",
    )(x, g, w1, w3, w2, gf)


def _odd_kernel(x_ref, g_ref, wuv_ref, gv_ref, ws_ref, bs_ref, wo_ref, o_ref, y_ref):
    x = x_ref[...]
    hn = _rms(x, g_ref[...]).astype(BF16)
    z = jnp.dot(hn, wuv_ref[...], preferred_element_type=F32)
    z = 0.5 * z * (1.0 + lax.erf(z * math.sqrt(0.5)))
    u = z[:, :SGU_WIDTH]
    vn = _rms(z[:, SGU_WIDTH:], gv_ref[...]).astype(BF16)
    gd = SGU_WIDTH // SGU_GROUPS
    for r in range(x.shape[0] // SGU_CHUNK):
        rows = slice(r * SGU_CHUNK, (r + 1) * SGU_CHUNK)
        for g in range(SGU_GROUPS):
            cols = slice(g * gd, (g + 1) * gd)
            sv = jnp.dot(ws_ref[g], vn[rows, cols], preferred_element_type=F32) + bs_ref[g]
            y_ref[rows, cols] = (u[rows, cols] * sv).astype(BF16)
    o_ref[...] = x + jnp.dot(y_ref[...], wo_ref[...], preferred_element_type=F32)


def _odd_mixer(x, g, wuv, gv, ws, bs, wo):
    m = x.shape[0]
    tm = ROW_TILE
    gd = SGU_WIDTH // SGU_GROUPS
    return pl.pallas_call(
        _odd_kernel,
        grid=(m // tm,),
        in_specs=[
            pl.BlockSpec((tm, D_MODEL), lambda i: (i, 0)),
            _const_spec((1, D_MODEL)),
            _const_spec((D_MODEL, 2 * SGU_WIDTH)),
            _const_spec((1, SGU_WIDTH)),
            _const_spec((SGU_GROUPS, SGU_CHUNK, SGU_CHUNK)),
            _const_spec((SGU_GROUPS, SGU_CHUNK, gd)),
            _const_spec((SGU_WIDTH, D_MODEL)),
        ],
        out_specs=pl.BlockSpec((tm, D_MODEL), lambda i: (i, 0)),
        out_shape=jax.ShapeDtypeStruct((m, D_MODEL), F32),
        scratch_shapes=[pltpu.VMEM((tm, SGU_WIDTH), BF16)],
        compiler_params=pltpu.CompilerParams(
            dimension_semantics=("arbitrary",), vmem_limit_bytes=VMEM_LIMIT),
        name="sgu_mixer",
    )(x, g, wuv, gv, ws, bs, wo)


def _chunk_cols(w):
    k, n = w.shape
    return jnp.transpose(w.reshape(k, n // FF_CHUNK, FF_CHUNK), (1, 0, 2))


def kernel(x, rel_bias_table, norm_mix_g, norm_ffn_g, even_w_in, even_w_out, diff_lambda,
           diff_subln_g, odd_w_uv, odd_v_norm_g, odd_w_s, odd_b_s, odd_w_out, ffn_w1, ffn_w3,
           ffn_w2, final_norm_g):
    batch, seq, d = x.shape
    assert (seq, d) == (SEQ_LEN, D_MODEL)
    depth = norm_mix_g.shape[0]
    m = batch * seq
    h = x.reshape(m, d)
    stage_a, stage_b, stage_c, cdft = _dft_constants()

    for i in range(depth):
        jx = i // 2
        g_mix = norm_mix_g[i].reshape(1, d)
        if i % 2 == 0:
            lam_init = 0.8 - 0.6 * math.exp(-0.3 * i)
            w_in = even_w_in[jx].astype(BF16)
            n_fqk = FOURIER_WIDTH + 2 * DIFF_WIDTH
            xcs, q, k, vt = _even_in(h, g_mix, w_in[:, :n_fqk], w_in[:, n_fqk:].T, cdft,
                                     batch, seq)
            f = _fourier(xcs, stage_a, stage_b, stage_c, batch)
            bias = _bias_tiles(rel_bias_table)
            att = _attention(q, k, vt, bias, diff_lambda[jx],
                             diff_subln_g[jx].reshape(DIFF_V_DIM, 1), lam_init, batch, seq)
            w_out = even_w_out[jx].astype(BF16)
            h = _even_out(h, f, att, w_out[:FOURIER_WIDTH], w_out[FOURIER_WIDTH:])
        else:
            bs = jnp.broadcast_to(odd_b_s[jx][:, :, None],
                                  (SGU_GROUPS, SGU_CHUNK, SGU_WIDTH // SGU_GROUPS))
            h = _odd_mixer(h, g_mix, odd_w_uv[jx].astype(BF16),
                           odd_v_norm_g[jx].reshape(1, SGU_WIDTH),
                           odd_w_s[jx].astype(BF16), bs, odd_w_out[jx].astype(BF16))
        last = i == depth - 1
        h = _ffn(h, norm_ffn_g[i].reshape(1, d),
                 _chunk_cols(ffn_w1[i].astype(BF16)), _chunk_cols(ffn_w3[i].astype(BF16)),
                 ffn_w2[i].astype(BF16).reshape(D_FF // FF_CHUNK, FF_CHUNK, d),
                 final_norm_g.reshape(1, d), last)
    return h.reshape(batch, seq, d)
```

```python
import functools
import math

import numpy as np
import jax
import jax.numpy as jnp
from jax import lax
from jax.experimental import pallas as pl
from jax.experimental.pallas import tpu as pltpu

F32 = jnp.float32
BF16 = jnp.bfloat16

D_MODEL = 1024
FOURIER_GROUP_DIM = 128
FOURIER_WIDTH = 512
DIFF_HEADS = 4
DIFF_QK_DIM = 64
DIFF_V_DIM = 128
DIFF_WIDTH = 512
REL_BUCKETS = 32
REL_MAX_DIST = 128
SGU_CHUNK = 128
SGU_GROUPS = 8
SGU_WIDTH = 1024
D_FF = 2816
RMS_EPS = 1e-6
LOG2E = math.log2(math.e)

DFT_RADIX = 16
SEQ_LEN = DFT_RADIX ** 3

ROW_TILE = 512
ATTN_TILE = 512
ATTN_STREAMS = 2
VT_ROWS = DIFF_V_DIM + 16
FF_CHUNK = 256
VMEM_LIMIT = 56 * 1024 * 1024


def _rms(x, g):
    return x * lax.rsqrt(jnp.mean(x * x, axis=-1, keepdims=True) + RMS_EPS) * g


def _const_spec(shape):
    return pl.BlockSpec(shape, lambda *_: (0,) * len(shape))


def _dft_constants():
    r = DFT_RADIX
    n = SEQ_LEN
    i = np.arange(r)
    f = np.exp(-2j * np.pi * np.outer(i, i) / r)
    eye = np.eye(r)

    def real_rep(g):
        return np.block([[g.real, -g.imag], [g.imag, g.real]])

    stage_a, stage_b = [], []
    for t in range(r):
        tw = np.exp(-2j * np.pi * t * i / (r * r))
        g = np.einsum('pr,kn,k->pknr', eye, f, tw).reshape(r * r, r * r)
        stage_a.append(real_rep(g))
    for rr in range(r):
        tw = np.exp(-2j * np.pi * rr * (i[None, :] + r * i[:, None]) / n)
        g = np.einsum('qk,jt,jq->jqtk', eye, f, tw).reshape(r * r, r * r)
        stage_b.append(real_rep(g))
    scale = 1.0 / math.sqrt(n * FOURIER_GROUP_DIM)
    g = np.einsum('qk,jr->jqrk', eye, f).reshape(r * r, r * r) * scale
    stage_c = np.concatenate([g.real, -g.imag], axis=1)
    cidx = np.arange(FOURIER_GROUP_DIM)
    cang = 2.0 * np.pi * np.outer(cidx, cidx) / FOURIER_GROUP_DIM
    eye2 = np.eye(2)
    cdft = np.stack([np.kron(eye2, np.cos(cang)), -np.kron(eye2, np.sin(cang))])
    return (jnp.asarray(np.stack(stage_a), BF16), jnp.asarray(np.stack(stage_b), BF16),
            jnp.asarray(stage_c, BF16), jnp.asarray(cdft, BF16))


def _even_in_kernel(x_ref, g_ref, w_ref, wvt_ref, cdft_ref, xcs_ref, q_ref, k_ref, vt_ref):
    hn = _rms(x_ref[...], g_ref[...]).astype(BF16)
    z = jnp.dot(hn, w_ref[...], preferred_element_type=F32)
    vt = lax.dot_general(wvt_ref[...], hn, (((1,), (1,)), ((), ())),
                         preferred_element_type=F32)
    pad_rows = lax.broadcasted_iota(jnp.int32, (VT_ROWS - DIFF_V_DIM, vt.shape[1]), 0)
    ones_tile = jnp.where(pad_rows == 0, 1.0, 0.0).astype(BF16)
    for hh in range(DIFF_HEADS):
        vt_ref[0, hh, 0, :DIFF_V_DIM] = vt[hh * DIFF_V_DIM:(hh + 1) * DIFF_V_DIM].astype(BF16)
        vt_ref[0, hh, 0, DIFF_V_DIM:] = ones_tile
    zf = z[:, :FOURIER_WIDTH].astype(BF16)
    for p in range(FOURIER_WIDTH // 256):
        zp = zf[:, p * 256:(p + 1) * 256]
        xcs_ref[0, 0, :, p * 256:(p + 1) * 256] = jnp.dot(
            zp, cdft_ref[0], preferred_element_type=F32).astype(BF16)
        xcs_ref[0, 1, :, p * 256:(p + 1) * 256] = jnp.dot(
            zp, cdft_ref[1], preferred_element_type=F32).astype(BF16)
    o = FOURIER_WIDTH
    q_ref[...] = (z[:, o:o + 512] * (DIFF_QK_DIM ** -0.5 * LOG2E)).astype(BF16)
    k_ref[...] = z[:, o + 512:o + 1024].astype(BF16)


def _even_in(x, g, w, wvt, cdft, batch, seq):
    m = x.shape[0]
    tm = ROW_TILE
    assert tm == ATTN_TILE
    per_b = seq // tm
    n_out = w.shape[1]
    return pl.pallas_call(
        _even_in_kernel,
        grid=(m // tm,),
        in_specs=[
            pl.BlockSpec((tm, D_MODEL), lambda i: (i, 0)),
            _const_spec((1, D_MODEL)),
            _const_spec((D_MODEL, n_out)),
            _const_spec((DIFF_WIDTH, D_MODEL)),
            _const_spec((2, 256, 256)),
        ],
        out_specs=[
            pl.BlockSpec((1, 2, tm, FOURIER_WIDTH), lambda i: (i // per_b, 0, i % per_b, 0)),
            pl.BlockSpec((tm, 512), lambda i: (i, 0)),
            pl.BlockSpec((tm, 512), lambda i: (i, 0)),
            pl.BlockSpec((1, DIFF_HEADS, 1, VT_ROWS, tm),
                         lambda i: (i // per_b, 0, i % per_b, 0, 0)),
        ],
        out_shape=[
            jax.ShapeDtypeStruct((batch, 2, seq, FOURIER_WIDTH), BF16),
            jax.ShapeDtypeStruct((m, 512), BF16),
            jax.ShapeDtypeStruct((m, 512), BF16),
            jax.ShapeDtypeStruct((batch, DIFF_HEADS, per_b, VT_ROWS, tm), BF16),
        ],
        compiler_params=pltpu.CompilerParams(
            dimension_semantics=("arbitrary",), vmem_limit_bytes=VMEM_LIMIT),
        name="even_in_proj",
    )(x, g, w, wvt, cdft)


def _dft_stage_kernel(x_ref, w_ref, o_ref):
    w = w_ref[0] if len(w_ref.shape) == 3 else w_ref[...]
    for b in range(x_ref.shape[0]):
        y = jnp.dot(w, x_ref[b].reshape(w.shape[1], FOURIER_WIDTH),
                    preferred_element_type=F32)
        o_ref[b] = y.astype(BF16).reshape(o_ref.shape[1:])


def _fourier(xcs, stage_a, stage_b, stage_c, batch):
    r = DFT_RADIX
    c = FOURIER_WIDTH
    nn = 2 * r * r
    params = pltpu.CompilerParams(
        dimension_semantics=("arbitrary",), vmem_limit_bytes=VMEM_LIMIT)
    fix_mid = pl.BlockSpec((batch, 2, r, None, r, c), lambda d: (0, 0, 0, d, 0, 0))
    fix_out = pl.BlockSpec((batch, 2, None, r, r, c), lambda d: (0, 0, d, 0, 0, 0))
    per_digit = pl.BlockSpec((1, nn, nn), lambda d: (d, 0, 0))
    planes = jax.ShapeDtypeStruct((batch, 2, r, r, r, c), BF16)
    x6 = xcs.reshape(batch, 2, r, r, r, c)
    y = pl.pallas_call(
        _dft_stage_kernel, grid=(r,), in_specs=[fix_mid, per_digit],
        out_specs=fix_mid, out_shape=planes, compiler_params=params,
        name="seq_dft_stage_a")(x6, stage_a)
    z = pl.pallas_call(
        _dft_stage_kernel, grid=(r,), in_specs=[fix_out, per_digit],
        out_specs=fix_out, out_shape=planes, compiler_params=params,
        name="seq_dft_stage_b")(y, stage_b)
    f = pl.pallas_call(
        _dft_stage_kernel, grid=(r,),
        in_specs=[fix_mid, _const_spec((r * r, nn))],
        out_specs=pl.BlockSpec((batch, r, None, r, c), lambda d: (0, 0, d, 0, 0)),
        out_shape=jax.ShapeDtypeStruct((batch, r, r, r, c), BF16), compiler_params=params,
        name="seq_dft_stage_c")(z, stage_c)
    return f.reshape(batch * SEQ_LEN, c)


def _bias_tiles_kernel(tab_ref, o_ref):
    h = pl.program_id(0)
    tt = ATTN_TILE
    half = REL_BUCKETS // 2
    max_exact = half // 2
    row = lax.broadcasted_iota(jnp.int32, (tt, tt), 0)
    col = lax.broadcasted_iota(jnp.int32, (tt, tt), 1)
    for t in range(5):
        rel = row - col + (t - 2) * tt
        ret = jnp.where(rel > 0, half, 0)
        n = jnp.abs(rel)
        nf = jnp.maximum(n, 1).astype(F32)
        large = max_exact + (jnp.log(nf / max_exact) / math.log(REL_MAX_DIST / max_exact)
                             * (half - max_exact)).astype(jnp.int32)
        large = jnp.minimum(large, half - 1)
        bucket = ret + jnp.where(n < max_exact, n, large)
        lo = half if t > 2 else 0
        hi = half if t < 2 else REL_BUCKETS
        acc = jnp.zeros((tt, tt), F32)
        for bk in range(lo, hi):
            acc = jnp.where(bucket == bk, tab_ref[bk, h], acc)
        o_ref[0, t] = acc * LOG2E


def _bias_tiles(table):
    tt = ATTN_TILE
    return pl.pallas_call(
        _bias_tiles_kernel,
        grid=(DIFF_HEADS,),
        in_specs=[pl.BlockSpec(memory_space=pltpu.SMEM)],
        out_specs=pl.BlockSpec((1, 5, tt, tt), lambda h: (h, 0, 0, 0)),
        out_shape=jax.ShapeDtypeStruct((DIFF_HEADS, 5, tt, tt), F32),
        compiler_params=pltpu.CompilerParams(
            dimension_semantics=("arbitrary",), vmem_limit_bytes=VMEM_LIMIT),
        name="rel_bias_tiles",
    )(table)


def _attn_kernel(tab_ref, q_ref, k_ref, vt_ref, bias_ref, lam_ref, g_ref, o_ref, *,
                 lam_init, n_kv):
    h = pl.program_id(1)
    i = pl.program_id(2)
    tt = ATTN_TILE
    q = q_ref[...].astype(F32)
    lane = lax.broadcasted_iota(jnp.int32, q.shape, 1)
    qm = [jnp.where(lane < DIFF_QK_DIM, q, 0.0).astype(BF16),
          jnp.where(lane >= DIFF_QK_DIM, q, 0.0).astype(BF16)]
    c_after = tab_ref[REL_BUCKETS - 1, h] * LOG2E
    c_before = tab_ref[REL_BUCKETS // 2 - 1, h] * LOG2E

    def scores(d, mp):
        j = lax.rem(i + d, n_kv)
        kt = k_ref[pl.ds(pl.multiple_of(j * tt, tt), tt), :]
        s = lax.dot_general(kt, qm[mp], (((1,), (1,)), ((), ())),
                            preferred_element_type=F32)
        if 2 <= d <= n_kv - 2:
            return s, jnp.where(i + d < n_kv, c_after, c_before)
        return s + bias_ref[0, jnp.clip(j - i, -2, 2) + 2], None

    ns = ATTN_STREAMS
    m = [[None] * ns for _ in range(2)]
    acc = [[None] * ns for _ in range(2)]
    units = [(d, mp) for d in range(n_kv) for mp in range(2)]
    nxt = scores(*units[0])
    for u, (d, mp) in enumerate(units):
        s, const = nxt
        if u + 1 < len(units):
            nxt = scores(*units[u + 1])
        st = d % ns
        vt = vt_ref[0, 0, lax.rem(i + d, n_kv)]
        smax = jnp.max(s, axis=0, keepdims=True)
        if const is not None:
            smax = smax + const
        if m[mp][st] is None:
            m_new = smax
        else:
            m_new = jnp.maximum(m[mp][st], smax)
        shift = m_new if const is None else m_new - const
        pv = jnp.dot(vt, jnp.exp2(s - shift).astype(BF16), preferred_element_type=F32)
        if m[mp][st] is None:
            acc[mp][st] = pv
        else:
            acc[mp][st] = jnp.exp2(m[mp][st] - m_new) * acc[mp][st] + pv
        m[mp][st] = m_new
    num = []
    for mp in range(2):
        mm = functools.reduce(jnp.maximum, m[mp])
        a = sum(jnp.exp2(ms - mm) * ac for ms, ac in zip(m[mp], acc[mp]))
        num.append(a[:DIFF_V_DIM] / a[DIFF_V_DIM:DIFF_V_DIM + 1])
    lf = lam_ref[...]
    lam = (jnp.exp(jnp.sum(lf[0:1] * lf[1:2], axis=-1, keepdims=True))
           - jnp.exp(jnp.sum(lf[2:3] * lf[3:4], axis=-1, keepdims=True)) + lam_init)
    o = num[0] - lam * num[1]
    y = (o * lax.rsqrt(jnp.mean(o * o, axis=0, keepdims=True) + RMS_EPS) * g_ref[...]
         * (1.0 - lam_init))
    o_ref[...] = y.T.astype(BF16)


def _attention(q, k, vt, bias, table, lam, subln_g, lam_init, batch, seq):
    tt = ATTN_TILE
    nq = seq // tt
    m = batch * seq
    kern = functools.partial(_attn_kernel, lam_init=lam_init, n_kv=nq)
    return pl.pallas_call(
        kern,
        grid=(batch, DIFF_HEADS, nq),
        in_specs=[
            pl.BlockSpec(memory_space=pltpu.SMEM),
            pl.BlockSpec((tt, DIFF_V_DIM), lambda b, h, i: (b * nq + i, h)),
            pl.BlockSpec((seq, DIFF_V_DIM), lambda b, h, i: (b, h)),
            pl.BlockSpec((1, 1, nq, VT_ROWS, tt), lambda b, h, i: (b, h, 0, 0, 0)),
            pl.BlockSpec((1, 5, tt, tt), lambda b, h, i: (h, 0, 0, 0)),
            _const_spec((4, DIFF_QK_DIM)),
            _const_spec((DIFF_V_DIM, 1)),
        ],
        out_specs=pl.BlockSpec((tt, DIFF_V_DIM), lambda b, h, i: (b * nq + i, h)),
        out_shape=jax.ShapeDtypeStruct((m, DIFF_WIDTH), BF16),
        compiler_params=pltpu.CompilerParams(
            dimension_semantics=("arbitrary", "arbitrary", "arbitrary"),
            vmem_limit_bytes=VMEM_LIMIT),
        name="diff_attention",
    )(table, q, k, vt, bias, lam, subln_g)


def _even_out_kernel(x_ref, f_ref, a_ref, wf_ref, wa_ref, o_ref):
    o_ref[...] = (x_ref[...]
                  + jnp.dot(f_ref[...], wf_ref[...], preferred_element_type=F32)
                  + jnp.dot(a_ref[...], wa_ref[...], preferred_element_type=F32))


def _even_out(x, f, a, wf, wa):
    m = x.shape[0]
    tm = ROW_TILE
    return pl.pallas_call(
        _even_out_kernel,
        grid=(m // tm,),
        in_specs=[
            pl.BlockSpec((tm, D_MODEL), lambda i: (i, 0)),
            pl.BlockSpec((tm, FOURIER_WIDTH), lambda i: (i, 0)),
            pl.BlockSpec((tm, DIFF_WIDTH), lambda i: (i, 0)),
            _const_spec((FOURIER_WIDTH, D_MODEL)),
            _const_spec((DIFF_WIDTH, D_MODEL)),
        ],
        out_specs=pl.BlockSpec((tm, D_MODEL), lambda i: (i, 0)),
        out_shape=jax.ShapeDtypeStruct((m, D_MODEL), F32),
        compiler_params=pltpu.CompilerParams(
            dimension_semantics=("arbitrary",), vmem_limit_bytes=VMEM_LIMIT),
        name="even_out_proj",
    )(x, f, a, wf, wa)


def _ffn_kernel(x_ref, g_ref, w1_ref, w3_ref, w2_ref, gf_ref, o_ref, *, final_norm):
    x = x_ref[...]
    hn = _rms(x, g_ref[...]).astype(BF16)
    acc = x
    for c in range(D_FF // FF_CHUNK):
        cols = slice(c * FF_CHUNK, (c + 1) * FF_CHUNK)
        a = jnp.dot(hn, w1_ref[:, cols], preferred_element_type=F32)
        b = jnp.dot(hn, w3_ref[:, cols], preferred_element_type=F32)
        u = (a * jax.nn.sigmoid(a) * b).astype(BF16)
        acc = acc + jnp.dot(u, w2_ref[cols, :], preferred_element_type=F32)
    if final_norm:
        acc = _rms(acc, gf_ref[...])
    o_ref[...] = acc


def _ffn(x, g, w1, w3, w2, gf, final_norm):
    m = x.shape[0]
    tm = ROW_TILE
    return pl.pallas_call(
        functools.partial(_ffn_kernel, final_norm=final_norm),
        grid=(m // tm,),
        in_specs=[
            pl.BlockSpec((tm, D_MODEL), lambda i: (i, 0)),
            _const_spec((1, D_MODEL)),
            _const_spec((D_MODEL, D_FF)),
            _const_spec((D_MODEL, D_FF)),
            _const_spec((D_FF, D_MODEL)),
            _const_spec((1, D_MODEL)),
        ],
        out_specs=pl.BlockSpec((tm, D_MODEL), lambda i: (i, 0)),
        out_shape=jax.ShapeDtypeStruct((m, D_MODEL), F32),
        compiler_params=pltpu.CompilerParams(
            dimension_semantics=("arbitrary",), vmem_limit_bytes=VMEM_LIMIT),
        name="swiglu_ffn",
    )(x, g, w1, w3, w2, gf)


def _odd_kernel(x_ref, g_ref, wuv_ref, gv_ref, ws_ref, bs_ref, wo_ref, o_ref, y_ref):
    x = x_ref[...]
    hn = _rms(x, g_ref[...]).astype(BF16)
    z = jnp.dot(hn, wuv_ref[...], preferred_element_type=F32)
    z = 0.5 * z * (1.0 + lax.erf(z * math.sqrt(0.5)))
    u = z[:, :SGU_WIDTH]
    vn = _rms(z[:, SGU_WIDTH:], gv_ref[...]).astype(BF16)
    gd = SGU_WIDTH // SGU_GROUPS
    for r in range(x.shape[0] // SGU_CHUNK):
        rows = slice(r * SGU_CHUNK, (r + 1) * SGU_CHUNK)
        for g in range(SGU_GROUPS):
            cols = slice(g * gd, (g + 1) * gd)
            sv = jnp.dot(ws_ref[g], vn[rows, cols], preferred_element_type=F32) + bs_ref[g]
            y_ref[rows, cols] = (u[rows, cols] * sv).astype(BF16)
    o_ref[...] = x + jnp.dot(y_ref[...], wo_ref[...], preferred_element_type=F32)


def _odd_mixer(x, g, wuv, gv, ws, bs, wo):
    m = x.shape[0]
    tm = ROW_TILE
    gd = SGU_WIDTH // SGU_GROUPS
    return pl.pallas_call(
        _odd_kernel,
        grid=(m // tm,),
        in_specs=[
            pl.BlockSpec((tm, D_MODEL), lambda i: (i, 0)),
            _const_spec((1, D_MODEL)),
            _const_spec((D_MODEL, 2 * SGU_WIDTH)),
            _const_spec((1, SGU_WIDTH)),
            _const_spec((SGU_GROUPS, SGU_CHUNK, SGU_CHUNK)),
            _const_spec((SGU_GROUPS, SGU_CHUNK, gd)),
            _const_spec((SGU_WIDTH, D_MODEL)),
        ],
        out_specs=pl.BlockSpec((tm, D_MODEL), lambda i: (i, 0)),
        out_shape=jax.ShapeDtypeStruct((m, D_MODEL), F32),
        scratch_shapes=[pltpu.VMEM((tm, SGU_WIDTH), BF16)],
        compiler_params=pltpu.CompilerParams(
            dimension_semantics=("arbitrary",), vmem_limit_bytes=VMEM_LIMIT),
        name="sgu_mixer",
    )(x, g, wuv, gv, ws, bs, wo)


def kernel(x, rel_bias_table, norm_mix_g, norm_ffn_g, even_w_in, even_w_out, diff_lambda,
           diff_subln_g, odd_w_uv, odd_v_norm_g, odd_w_s, odd_b_s, odd_w_out, ffn_w1, ffn_w3,
           ffn_w2, final_norm_g):
    batch, seq, d = x.shape
    assert (seq, d) == (SEQ_LEN, D_MODEL)
    depth = norm_mix_g.shape[0]
    m = batch * seq
    h = x.reshape(m, d)
    stage_a, stage_b, stage_c, cdft = _dft_constants()

    for i in range(depth):
        jx = i // 2
        g_mix = norm_mix_g[i].reshape(1, d)
        if i % 2 == 0:
            lam_init = 0.8 - 0.6 * math.exp(-0.3 * i)
            w_in = even_w_in[jx].astype(BF16)
            n_fqk = FOURIER_WIDTH + 2 * DIFF_WIDTH
            xcs, q, k, vt = _even_in(h, g_mix, w_in[:, :n_fqk], w_in[:, n_fqk:].T, cdft,
                                     batch, seq)
            f = _fourier(xcs, stage_a, stage_b, stage_c, batch)
            bias = _bias_tiles(rel_bias_table)
            att = _attention(q, k, vt, bias, rel_bias_table, diff_lambda[jx],
                             diff_subln_g[jx].reshape(DIFF_V_DIM, 1), lam_init, batch, seq)
            w_out = even_w_out[jx].astype(BF16)
            h = _even_out(h, f, att, w_out[:FOURIER_WIDTH], w_out[FOURIER_WIDTH:])
        else:
            bs = jnp.broadcast_to(odd_b_s[jx][:, :, None],
                                  (SGU_GROUPS, SGU_CHUNK, SGU_WIDTH // SGU_GROUPS))
            h = _odd_mixer(h, g_mix, odd_w_uv[jx].astype(BF16),
                           odd_v_norm_g[jx].reshape(1, SGU_WIDTH),
                           odd_w_s[jx].astype(BF16), bs, odd_w_out[jx].astype(BF16))
        last = i == depth - 1
        h = _ffn(h, norm_ffn_g[i].reshape(1, d),
                 ffn_w1[i].astype(BF16), ffn_w3[i].astype(BF16), ffn_w2[i].astype(BF16),
                 final_norm_g.reshape(1, d), last)
    return h.reshape(batch, seq, d)
```

```python
import functools
import math

import numpy as np
import jax
import jax.numpy as jnp
from jax import lax
from jax.experimental import pallas as pl
from jax.experimental.pallas import tpu as pltpu

F32 = jnp.float32
BF16 = jnp.bfloat16

D_MODEL = 1024
FOURIER_GROUP_DIM = 128
FOURIER_WIDTH = 512
DIFF_HEADS = 4
DIFF_QK_DIM = 64
DIFF_V_DIM = 128
DIFF_WIDTH = 512
REL_BUCKETS = 32
REL_MAX_DIST = 128
SGU_CHUNK = 128
SGU_GROUPS = 8
SGU_WIDTH = 1024
D_FF = 2816
RMS_EPS = 1e-6
LOG2E = math.log2(math.e)

DFT_RADIX = 16
SEQ_LEN = DFT_RADIX ** 3

ROW_TILE = 512
EVEN_TAIL_BATCHES = 2
ATTN_TILE = 512
ATTN_STREAMS = 2
VT_ROWS = DIFF_V_DIM + 16
FF_CHUNK = 256
VMEM_LIMIT = 56 * 1024 * 1024


def _rms(x, g):
    return x * lax.rsqrt(jnp.mean(x * x, axis=-1, keepdims=True) + RMS_EPS) * g


def _const_spec(shape):
    return pl.BlockSpec(shape, lambda *_: (0,) * len(shape))


def _dft_constants():
    r = DFT_RADIX
    n = SEQ_LEN
    i = np.arange(r)
    f = np.exp(-2j * np.pi * np.outer(i, i) / r)
    eye = np.eye(r)

    def real_rep(g):
        return np.block([[g.real, -g.imag], [g.imag, g.real]])

    stage_a, stage_b = [], []
    for t in range(r):
        tw = np.exp(-2j * np.pi * t * i / (r * r))
        g = np.einsum('pr,kn,k->pknr', eye, f, tw).reshape(r * r, r * r)
        stage_a.append(real_rep(g))
    for rr in range(r):
        tw = np.exp(-2j * np.pi * rr * (i[None, :] + r * i[:, None]) / n)
        g = np.einsum('qk,jt,jq->jqtk', eye, f, tw).reshape(r * r, r * r)
        stage_b.append(real_rep(g))
    scale = 1.0 / math.sqrt(n * FOURIER_GROUP_DIM)
    g = np.einsum('qk,jr->jqrk', eye, f).reshape(r * r, r * r) * scale
    stage_c = np.concatenate([g.real, -g.imag], axis=1)
    cidx = np.arange(FOURIER_GROUP_DIM)
    cang = 2.0 * np.pi * np.outer(cidx, cidx) / FOURIER_GROUP_DIM
    eye2 = np.eye(2)
    cdft = np.stack([np.kron(eye2, np.cos(cang)), -np.kron(eye2, np.sin(cang))])
    return (jnp.asarray(np.stack(stage_a), BF16), jnp.asarray(np.stack(stage_b), BF16),
            jnp.asarray(stage_c, BF16), jnp.asarray(cdft, BF16))


def _even_in_kernel(x_ref, g_ref, w_ref, wvt_ref, cdft_ref, xcs_ref, q_ref, k_ref, vt_ref):
    hn = _rms(x_ref[...], g_ref[...]).astype(BF16)
    z = jnp.dot(hn, w_ref[...], preferred_element_type=F32)
    vt = lax.dot_general(wvt_ref[...], hn, (((1,), (1,)), ((), ())),
                         preferred_element_type=F32)
    pad_rows = lax.broadcasted_iota(jnp.int32, (VT_ROWS - DIFF_V_DIM, vt.shape[1]), 0)
    ones_tile = jnp.where(pad_rows == 0, 1.0, 0.0).astype(BF16)
    for hh in range(DIFF_HEADS):
        vt_ref[0, hh, 0, :DIFF_V_DIM] = vt[hh * DIFF_V_DIM:(hh + 1) * DIFF_V_DIM].astype(BF16)
        vt_ref[0, hh, 0, DIFF_V_DIM:] = ones_tile
    zf = z[:, :FOURIER_WIDTH].astype(BF16)
    for p in range(FOURIER_WIDTH // 256):
        zp = zf[:, p * 256:(p + 1) * 256]
        xcs_ref[0, 0, :, p * 256:(p + 1) * 256] = jnp.dot(
            zp, cdft_ref[0], preferred_element_type=F32).astype(BF16)
        xcs_ref[0, 1, :, p * 256:(p + 1) * 256] = jnp.dot(
            zp, cdft_ref[1], preferred_element_type=F32).astype(BF16)
    o = FOURIER_WIDTH
    q_ref[...] = (z[:, o:o + 512] * (DIFF_QK_DIM ** -0.5 * LOG2E)).astype(BF16)
    k_ref[...] = z[:, o + 512:o + 1024].astype(BF16)


def _even_in(x, g, w, wvt, cdft, batch, seq):
    m = x.shape[0]
    tm = ROW_TILE
    assert tm == ATTN_TILE
    per_b = seq // tm
    n_out = w.shape[1]
    return pl.pallas_call(
        _even_in_kernel,
        grid=(m // tm,),
        in_specs=[
            pl.BlockSpec((tm, D_MODEL), lambda i: (i, 0)),
            _const_spec((1, D_MODEL)),
            _const_spec((D_MODEL, n_out)),
            _const_spec((DIFF_WIDTH, D_MODEL)),
            _const_spec((2, 256, 256)),
        ],
        out_specs=[
            pl.BlockSpec((1, 2, tm, FOURIER_WIDTH), lambda i: (i // per_b, 0, i % per_b, 0)),
            pl.BlockSpec((tm, 512), lambda i: (i, 0)),
            pl.BlockSpec((tm, 512), lambda i: (i, 0)),
            pl.BlockSpec((1, DIFF_HEADS, 1, VT_ROWS, tm),
                         lambda i: (i // per_b, 0, i % per_b, 0, 0)),
        ],
        out_shape=[
            jax.ShapeDtypeStruct((batch, 2, seq, FOURIER_WIDTH), BF16),
            jax.ShapeDtypeStruct((m, 512), BF16),
            jax.ShapeDtypeStruct((m, 512), BF16),
            jax.ShapeDtypeStruct((batch, DIFF_HEADS, per_b, VT_ROWS, tm), BF16),
        ],
        compiler_params=pltpu.CompilerParams(
            dimension_semantics=("arbitrary",), vmem_limit_bytes=VMEM_LIMIT),
        name="even_in_proj",
    )(x, g, w, wvt, cdft)


def _dft_stage_kernel(x_ref, w_ref, o_ref):
    w = w_ref[0] if len(w_ref.shape) == 3 else w_ref[...]
    for b in range(x_ref.shape[0]):
        y = jnp.dot(w, x_ref[b].reshape(w.shape[1], FOURIER_WIDTH),
                    preferred_element_type=F32)
        o_ref[b] = y.astype(BF16).reshape(o_ref.shape[1:])


def _fourier_ab(xcs, stage_a, stage_b, batch):
    r = DFT_RADIX
    c = FOURIER_WIDTH
    nn = 2 * r * r
    params = pltpu.CompilerParams(
        dimension_semantics=("arbitrary",), vmem_limit_bytes=VMEM_LIMIT)
    fix_mid = pl.BlockSpec((batch, 2, r, None, r, c), lambda d: (0, 0, 0, d, 0, 0))
    fix_out = pl.BlockSpec((batch, 2, None, r, r, c), lambda d: (0, 0, d, 0, 0, 0))
    per_digit = pl.BlockSpec((1, nn, nn), lambda d: (d, 0, 0))
    planes = jax.ShapeDtypeStruct((batch, 2, r, r, r, c), BF16)
    x6 = xcs.reshape(batch, 2, r, r, r, c)
    y = pl.pallas_call(
        _dft_stage_kernel, grid=(r,), in_specs=[fix_mid, per_digit],
        out_specs=fix_mid, out_shape=planes, compiler_params=params,
        name="seq_dft_stage_a")(x6, stage_a)
    z = pl.pallas_call(
        _dft_stage_kernel, grid=(r,), in_specs=[fix_out, per_digit],
        out_specs=fix_out, out_shape=planes, compiler_params=params,
        name="seq_dft_stage_b")(y, stage_b)
    return z


def _bias_tiles_kernel(tab_ref, o_ref):
    h = pl.program_id(0)
    tt = ATTN_TILE
    half = REL_BUCKETS // 2
    max_exact = half // 2
    row = lax.broadcasted_iota(jnp.int32, (tt, tt), 0)
    col = lax.broadcasted_iota(jnp.int32, (tt, tt), 1)
    o_ref[0, 0] = jnp.full((tt, tt), tab_ref[half - 1, h] * LOG2E, F32)
    o_ref[0, 4] = jnp.full((tt, tt), tab_ref[REL_BUCKETS - 1, h] * LOG2E, F32)
    for t in range(1, 4):
        rel = row - col + (t - 2) * tt
        ret = jnp.where(rel > 0, half, 0)
        n = jnp.abs(rel)
        nf = jnp.maximum(n, 1).astype(F32)
        large = max_exact + (jnp.log(nf / max_exact) / math.log(REL_MAX_DIST / max_exact)
                             * (half - max_exact)).astype(jnp.int32)
        large = jnp.minimum(large, half - 1)
        bucket = ret + jnp.where(n < max_exact, n, large)
        lo = half if t > 2 else 0
        hi = half if t < 2 else REL_BUCKETS
        acc = jnp.zeros((tt, tt), F32)
        for bk in range(lo, hi):
            acc = jnp.where(bucket == bk, tab_ref[bk, h], acc)
        o_ref[0, t] = acc * LOG2E


def _bias_tiles(table):
    tt = ATTN_TILE
    return pl.pallas_call(
        _bias_tiles_kernel,
        grid=(DIFF_HEADS,),
        in_specs=[pl.BlockSpec(memory_space=pltpu.SMEM)],
        out_specs=pl.BlockSpec((1, 5, tt, tt), lambda h: (h, 0, 0, 0)),
        out_shape=jax.ShapeDtypeStruct((DIFF_HEADS, 5, tt, tt), F32),
        compiler_params=pltpu.CompilerParams(
            dimension_semantics=("arbitrary",), vmem_limit_bytes=VMEM_LIMIT),
        name="rel_bias_tiles",
    )(table)


def _attn_kernel(tab_ref, q_ref, k_ref, vt_ref, bias_ref, lam_ref, g_ref, o_ref, *,
                 lam_init, n_kv):
    h = pl.program_id(0)
    i = pl.program_id(2)
    tt = ATTN_TILE
    q = q_ref[...].astype(F32)
    lane = lax.broadcasted_iota(jnp.int32, q.shape, 1)
    qm = [jnp.where(lane < DIFF_QK_DIM, q, 0.0).astype(BF16),
          jnp.where(lane >= DIFF_QK_DIM, q, 0.0).astype(BF16)]
    c_after = tab_ref[REL_BUCKETS - 1, h] * LOG2E
    c_before = tab_ref[REL_BUCKETS // 2 - 1, h] * LOG2E

    def scores(d, mp):
        j = lax.rem(i + d, n_kv)
        kt = k_ref[pl.ds(pl.multiple_of(j * tt, tt), tt), :]
        s = lax.dot_general(kt, qm[mp], (((1,), (1,)), ((), ())),
                            preferred_element_type=F32)
        if 2 <= d <= n_kv - 2:
            return s, jnp.where(i + d < n_kv, c_after, c_before)
        return s + bias_ref[0, jnp.clip(j - i, -2, 2) + 2], None

    ns = ATTN_STREAMS
    m = [[None] * ns for _ in range(2)]
    acc = [[None] * ns for _ in range(2)]
    units = [(d, mp) for d in range(n_kv) for mp in range(2)]
    nxt = scores(*units[0])
    for u, (d, mp) in enumerate(units):
        s, const = nxt
        if u + 1 < len(units):
            nxt = scores(*units[u + 1])
        st = d % ns
        vt = vt_ref[0, 0, lax.rem(i + d, n_kv)]
        smax = jnp.max(s, axis=0, keepdims=True)
        if const is not None:
            smax = smax + const
        if m[mp][st] is None:
            m_new = smax
        else:
            m_new = jnp.maximum(m[mp][st], smax)
        shift = m_new if const is None else m_new - const
        pv = jnp.dot(vt, jnp.exp2(s - shift).astype(BF16), preferred_element_type=F32)
        if m[mp][st] is None:
            acc[mp][st] = pv
        else:
            acc[mp][st] = jnp.exp2(m[mp][st] - m_new) * acc[mp][st] + pv
        m[mp][st] = m_new
    num = []
    for mp in range(2):
        mm = functools.reduce(jnp.maximum, m[mp])
        a = sum(jnp.exp2(ms - mm) * ac for ms, ac in zip(m[mp], acc[mp]))
        num.append(a[:DIFF_V_DIM] / a[DIFF_V_DIM:DIFF_V_DIM + 1])
    lf = lam_ref[...]
    lam = (jnp.exp(jnp.sum(lf[0:1] * lf[1:2], axis=-1, keepdims=True))
           - jnp.exp(jnp.sum(lf[2:3] * lf[3:4], axis=-1, keepdims=True)) + lam_init)
    o = num[0] - lam * num[1]
    y = (o * lax.rsqrt(jnp.mean(o * o, axis=0, keepdims=True) + RMS_EPS) * g_ref[...]
         * (1.0 - lam_init))
    o_ref[...] = y.T.astype(BF16)


def _attention(q, k, vt, bias, table, lam, subln_g, lam_init, batch, seq):
    tt = ATTN_TILE
    nq = seq // tt
    m = batch * seq
    kern = functools.partial(_attn_kernel, lam_init=lam_init, n_kv=nq)
    return pl.pallas_call(
        kern,
        grid=(DIFF_HEADS, batch, nq),
        in_specs=[
            pl.BlockSpec(memory_space=pltpu.SMEM),
            pl.BlockSpec((tt, DIFF_V_DIM), lambda h, b, i: (b * nq + i, h)),
            pl.BlockSpec((seq, DIFF_V_DIM), lambda h, b, i: (b, h)),
            pl.BlockSpec((1, 1, nq, VT_ROWS, tt), lambda h, b, i: (b, h, 0, 0, 0)),
            pl.BlockSpec((1, 5, tt, tt), lambda h, b, i: (h, 0, 0, 0)),
            _const_spec((4, DIFF_QK_DIM)),
            _const_spec((DIFF_V_DIM, 1)),
        ],
        out_specs=pl.BlockSpec((tt, DIFF_V_DIM), lambda h, b, i: (b * nq + i, h)),
        out_shape=jax.ShapeDtypeStruct((m, DIFF_WIDTH), BF16),
        compiler_params=pltpu.CompilerParams(
            dimension_semantics=("arbitrary", "arbitrary", "arbitrary"),
            vmem_limit_bytes=VMEM_LIMIT),
        name="diff_attention",
    )(table, q, k, vt, bias, lam, subln_g)


def _swiglu_residual(x, g_ref, w1_ref, w3_ref, w2_ref):
    hn = _rms(x, g_ref[...]).astype(BF16)
    acc = x
    for c in range(D_FF // FF_CHUNK):
        cols = slice(c * FF_CHUNK, (c + 1) * FF_CHUNK)
        a = jnp.dot(hn, w1_ref[:, cols], preferred_element_type=F32)
        b = jnp.dot(hn, w3_ref[:, cols], preferred_element_type=F32)
        u = (a * jax.nn.sigmoid(a) * b).astype(BF16)
        acc = acc + jnp.dot(u, w2_ref[cols, :], preferred_element_type=F32)
    return acc


def _ffn_kernel(x_ref, g_ref, w1_ref, w3_ref, w2_ref, gf_ref, o_ref, *, final_norm):
    acc = _swiglu_residual(x_ref[...], g_ref, w1_ref, w3_ref, w2_ref)
    if final_norm:
        acc = _rms(acc, gf_ref[...])
    o_ref[...] = acc


def _ffn(x, g, w1, w3, w2, gf, final_norm):
    m = x.shape[0]
    tm = ROW_TILE
    return pl.pallas_call(
        functools.partial(_ffn_kernel, final_norm=final_norm),
        grid=(m // tm,),
        in_specs=[
            pl.BlockSpec((tm, D_MODEL), lambda i: (i, 0)),
            _const_spec((1, D_MODEL)),
            _const_spec((D_MODEL, D_FF)),
            _const_spec((D_MODEL, D_FF)),
            _const_spec((D_FF, D_MODEL)),
            _const_spec((1, D_MODEL)),
        ],
        out_specs=pl.BlockSpec((tm, D_MODEL), lambda i: (i, 0)),
        out_shape=jax.ShapeDtypeStruct((m, D_MODEL), F32),
        compiler_params=pltpu.CompilerParams(
            dimension_semantics=("arbitrary",), vmem_limit_bytes=VMEM_LIMIT),
        name="swiglu_ffn",
    )(x, g, w1, w3, w2, gf)


def _even_tail_kernel(z_ref, x_ref, a_ref, sc_ref, wo_ref, g_ref, w1_ref, w3_ref, w2_ref,
                      o_ref):
    nb = z_ref.shape[0]
    rows = nb * DFT_RADIX * DFT_RADIX
    f = jnp.concatenate(
        [jnp.dot(sc_ref[...], z_ref[b].reshape(sc_ref.shape[1], FOURIER_WIDTH),
                 preferred_element_type=F32).astype(BF16) for b in range(nb)], axis=0)
    x = x_ref[...].reshape(rows, D_MODEL)
    mix = jnp.concatenate([f, a_ref[...].reshape(rows, DIFF_WIDTH)], axis=1)
    h = x + jnp.dot(mix, wo_ref[...], preferred_element_type=F32)
    o_ref[...] = _swiglu_residual(h, g_ref, w1_ref, w3_ref, w2_ref).reshape(o_ref.shape)


def _even_tail(z, x, att, stage_c, w_out, g, w1, w3, w2, batch):
    r = DFT_RADIX
    nb = EVEN_TAIL_BATCHES
    c = FOURIER_WIDTH

    def tiles(width):
        return pl.BlockSpec((nb, r, None, r, width), lambda d, b: (b, 0, d, 0, 0))

    out = pl.pallas_call(
        _even_tail_kernel,
        grid=(r, batch // nb),
        in_specs=[
            pl.BlockSpec((nb, 2, r, None, r, c), lambda d, b: (b, 0, 0, d, 0, 0)),
            tiles(D_MODEL),
            tiles(DIFF_WIDTH),
            _const_spec((r * r, 2 * r * r)),
            _const_spec((FOURIER_WIDTH + DIFF_WIDTH, D_MODEL)),
            _const_spec((1, D_MODEL)),
            _const_spec((D_MODEL, D_FF)),
            _const_spec((D_MODEL, D_FF)),
            _const_spec((D_FF, D_MODEL)),
        ],
        out_specs=tiles(D_MODEL),
        out_shape=jax.ShapeDtypeStruct((batch, r, r, r, D_MODEL), F32),
        compiler_params=pltpu.CompilerParams(
            dimension_semantics=("arbitrary", "arbitrary"), vmem_limit_bytes=VMEM_LIMIT),
        name="even_tail",
    )(z, x.reshape(batch, r, r, r, D_MODEL), att.reshape(batch, r, r, r, DIFF_WIDTH),
      stage_c, w_out, g, w1, w3, w2)
    return out.reshape(batch * SEQ_LEN, D_MODEL)


def _odd_kernel(x_ref, g_ref, wuv_ref, gv_ref, ws_ref, bs_ref, wo_ref, o_ref, y_ref):
    x = x_ref[...]
    hn = _rms(x, g_ref[...]).astype(BF16)
    z = jnp.dot(hn, wuv_ref[...], preferred_element_type=F32)
    z = 0.5 * z * (1.0 + lax.erf(z * math.sqrt(0.5)))
    u = z[:, :SGU_WIDTH]
    vn = _rms(z[:, SGU_WIDTH:], gv_ref[...]).astype(BF16)
    gd = SGU_WIDTH // SGU_GROUPS
    for r in range(x.shape[0] // SGU_CHUNK):
        rows = slice(r * SGU_CHUNK, (r + 1) * SGU_CHUNK)
        for g in range(SGU_GROUPS):
            cols = slice(g * gd, (g + 1) * gd)
            sv = jnp.dot(ws_ref[g], vn[rows, cols], preferred_element_type=F32) + bs_ref[g]
            y_ref[rows, cols] = (u[rows, cols] * sv).astype(BF16)
    o_ref[...] = x + jnp.dot(y_ref[...], wo_ref[...], preferred_element_type=F32)


def _odd_mixer(x, g, wuv, gv, ws, bs, wo):
    m = x.shape[0]
    tm = ROW_TILE
    gd = SGU_WIDTH // SGU_GROUPS
    return pl.pallas_call(
        _odd_kernel,
        grid=(m // tm,),
        in_specs=[
            pl.BlockSpec((tm, D_MODEL), lambda i: (i, 0)),
            _const_spec((1, D_MODEL)),
            _const_spec((D_MODEL, 2 * SGU_WIDTH)),
            _const_spec((1, SGU_WIDTH)),
            _const_spec((SGU_GROUPS, SGU_CHUNK, SGU_CHUNK)),
            _const_spec((SGU_GROUPS, SGU_CHUNK, gd)),
            _const_spec((SGU_WIDTH, D_MODEL)),
        ],
        out_specs=pl.BlockSpec((tm, D_MODEL), lambda i: (i, 0)),
        out_shape=jax.ShapeDtypeStruct((m, D_MODEL), F32),
        scratch_shapes=[pltpu.VMEM((tm, SGU_WIDTH), BF16)],
        compiler_params=pltpu.CompilerParams(
            dimension_semantics=("arbitrary",), vmem_limit_bytes=VMEM_LIMIT),
        name="sgu_mixer",
    )(x, g, wuv, gv, ws, bs, wo)


def kernel(x, rel_bias_table, norm_mix_g, norm_ffn_g, even_w_in, even_w_out, diff_lambda,
           diff_subln_g, odd_w_uv, odd_v_norm_g, odd_w_s, odd_b_s, odd_w_out, ffn_w1, ffn_w3,
           ffn_w2, final_norm_g):
    batch, seq, d = x.shape
    assert (seq, d) == (SEQ_LEN, D_MODEL)
    depth = norm_mix_g.shape[0]
    assert depth % 2 == 0
    m = batch * seq
    h = x.reshape(m, d)
    stage_a, stage_b, stage_c, cdft = _dft_constants()

    for i in range(depth):
        jx = i // 2
        g_mix = norm_mix_g[i].reshape(1, d)
        if i % 2 == 0:
            lam_init = 0.8 - 0.6 * math.exp(-0.3 * i)
            w_in = even_w_in[jx].astype(BF16)
            n_fqk = FOURIER_WIDTH + 2 * DIFF_WIDTH
            xcs, q, k, vt = _even_in(h, g_mix, w_in[:, :n_fqk], w_in[:, n_fqk:].T, cdft,
                                     batch, seq)
            z = _fourier_ab(xcs, stage_a, stage_b, batch)
            bias = _bias_tiles(rel_bias_table)
            att = _attention(q, k, vt, bias, rel_bias_table, diff_lambda[jx],
                             diff_subln_g[jx].reshape(DIFF_V_DIM, 1), lam_init, batch, seq)
            h = _even_tail(z, h, att, stage_c, even_w_out[jx].astype(BF16),
                           norm_ffn_g[i].reshape(1, d), ffn_w1[i].astype(BF16),
                           ffn_w3[i].astype(BF16), ffn_w2[i].astype(BF16), batch)
            continue
        else:
            bs = jnp.broadcast_to(odd_b_s[jx][:, :, None],
                                  (SGU_GROUPS, SGU_CHUNK, SGU_WIDTH // SGU_GROUPS))
            h = _odd_mixer(h, g_mix, odd_w_uv[jx].astype(BF16),
                           odd_v_norm_g[jx].reshape(1, SGU_WIDTH),
                           odd_w_s[jx].astype(BF16), bs, odd_w_out[jx].astype(BF16))
        h = _ffn(h, norm_ffn_g[i].reshape(1, d),
                 ffn_w1[i].astype(BF16), ffn_w3[i].astype(BF16), ffn_w2[i].astype(BF16),
                 final_norm_g.reshape(1, d), i == depth - 1)
    return h.reshape(batch, seq, d)
```

```python
import functools
import math

import numpy as np
import jax
import jax.numpy as jnp
from jax import lax
from jax.experimental import pallas as pl
from jax.experimental.pallas import tpu as pltpu

F32 = jnp.float32
BF16 = jnp.bfloat16

D_MODEL = 1024
FOURIER_GROUP_DIM = 128
FOURIER_WIDTH = 512
DIFF_HEADS = 4
DIFF_QK_DIM = 64
DIFF_V_DIM = 128
DIFF_WIDTH = 512
REL_BUCKETS = 32
REL_MAX_DIST = 128
SGU_CHUNK = 128
SGU_GROUPS = 8
SGU_WIDTH = 1024
D_FF = 2816
RMS_EPS = 1e-6
LOG2E = math.log2(math.e)

DFT_RADIX = 16
SEQ_LEN = DFT_RADIX ** 3

ROW_TILE = 512
EVEN_TAIL_BATCHES = 2
ATTN_TILE = 512
ATTN_LOOKAHEAD = 2
VT_ROWS = DIFF_V_DIM + 16
FF_CHUNK = 256
VMEM_LIMIT = 56 * 1024 * 1024


def _rms(x, g):
    return x * lax.rsqrt(jnp.mean(x * x, axis=-1, keepdims=True) + RMS_EPS) * g


def _const_spec(shape):
    return pl.BlockSpec(shape, lambda *_: (0,) * len(shape))


def _dft_constants():
    r = DFT_RADIX
    n = SEQ_LEN
    i = np.arange(r)
    f = np.exp(-2j * np.pi * np.outer(i, i) / r)
    eye = np.eye(r)

    def real_rep(g):
        return np.block([[g.real, -g.imag], [g.imag, g.real]])

    stage_a, stage_b = [], []
    for t in range(r):
        tw = np.exp(-2j * np.pi * t * i / (r * r))
        g = np.einsum('pr,kn,k->pknr', eye, f, tw).reshape(r * r, r * r)
        stage_a.append(real_rep(g))
    for rr in range(r):
        tw = np.exp(-2j * np.pi * rr * (i[None, :] + r * i[:, None]) / n)
        g = np.einsum('qk,jt,jq->jqtk', eye, f, tw).reshape(r * r, r * r)
        stage_b.append(real_rep(g))
    scale = 1.0 / math.sqrt(n * FOURIER_GROUP_DIM)
    g = np.einsum('qk,jr->jqrk', eye, f).reshape(r * r, r * r) * scale
    stage_c = np.concatenate([g.real, -g.imag], axis=1)
    cidx = np.arange(FOURIER_GROUP_DIM)
    cang = 2.0 * np.pi * np.outer(cidx, cidx) / FOURIER_GROUP_DIM
    eye2 = np.eye(2)
    cdft = np.stack([np.kron(eye2, np.cos(cang)), -np.kron(eye2, np.sin(cang))])
    return (jnp.asarray(np.stack(stage_a), BF16), jnp.asarray(np.stack(stage_b), BF16),
            jnp.asarray(stage_c, BF16), jnp.asarray(cdft, BF16))


def _even_in_kernel(x_ref, g_ref, w_ref, wvt_ref, cdft_ref, xcs_ref, q_ref, k_ref, vt_ref):
    hn = _rms(x_ref[...], g_ref[...]).astype(BF16)
    n_fqk = FOURIER_WIDTH + 2 * DIFF_WIDTH
    z = jnp.dot(hn, w_ref[:, :n_fqk], preferred_element_type=F32)
    vt = lax.dot_general(wvt_ref[...], hn, (((1,), (1,)), ((), ())),
                         preferred_element_type=F32)
    pad_rows = lax.broadcasted_iota(jnp.int32, (VT_ROWS - DIFF_V_DIM, vt.shape[1]), 0)
    ones_tile = jnp.where(pad_rows == 0, 1.0, 0.0).astype(BF16)
    for hh in range(DIFF_HEADS):
        vt_ref[0, hh, 0, :DIFF_V_DIM] = vt[hh * DIFF_V_DIM:(hh + 1) * DIFF_V_DIM].astype(BF16)
        vt_ref[0, hh, 0, DIFF_V_DIM:] = ones_tile
    zf = z[:, :FOURIER_WIDTH].astype(BF16)
    for p in range(FOURIER_WIDTH // 256):
        zp = zf[:, p * 256:(p + 1) * 256]
        xcs_ref[0, 0, :, p * 256:(p + 1) * 256] = jnp.dot(
            zp, cdft_ref[0], preferred_element_type=F32).astype(BF16)
        xcs_ref[0, 1, :, p * 256:(p + 1) * 256] = jnp.dot(
            zp, cdft_ref[1], preferred_element_type=F32).astype(BF16)
    o = FOURIER_WIDTH
    q_ref[...] = (z[:, o:o + 512] * (DIFF_QK_DIM ** -0.5 * LOG2E)).astype(BF16)
    k_ref[...] = z[:, o + 512:o + 1024].astype(BF16)


def _even_in(x, g, w, wvt, cdft, batch, seq):
    m = x.shape[0]
    tm = ROW_TILE
    assert tm == ATTN_TILE
    per_b = seq // tm
    n_out = w.shape[1]
    return pl.pallas_call(
        _even_in_kernel,
        grid=(m // tm,),
        in_specs=[
            pl.BlockSpec((tm, D_MODEL), lambda i: (i, 0)),
            _const_spec((1, D_MODEL)),
            _const_spec((D_MODEL, n_out)),
            _const_spec((DIFF_WIDTH, D_MODEL)),
            _const_spec((2, 256, 256)),
        ],
        out_specs=[
            pl.BlockSpec((1, 2, tm, FOURIER_WIDTH), lambda i: (i // per_b, 0, i % per_b, 0)),
            pl.BlockSpec((tm, 512), lambda i: (i, 0)),
            pl.BlockSpec((tm, 512), lambda i: (i, 0)),
            pl.BlockSpec((1, DIFF_HEADS, 1, VT_ROWS, tm),
                         lambda i: (i // per_b, 0, i % per_b, 0, 0)),
        ],
        out_shape=[
            jax.ShapeDtypeStruct((batch, 2, seq, FOURIER_WIDTH), BF16),
            jax.ShapeDtypeStruct((m, 512), BF16),
            jax.ShapeDtypeStruct((m, 512), BF16),
            jax.ShapeDtypeStruct((batch, DIFF_HEADS, per_b, VT_ROWS, tm), BF16),
        ],
        compiler_params=pltpu.CompilerParams(
            dimension_semantics=("arbitrary",), vmem_limit_bytes=VMEM_LIMIT),
        name="even_in_proj",
    )(x, g, w, wvt, cdft)


def _dft_stage_kernel(x_ref, w_ref, o_ref):
    w = w_ref[0] if len(w_ref.shape) == 3 else w_ref[...]
    for b in range(x_ref.shape[0]):
        y = jnp.dot(w, x_ref[b].reshape(w.shape[1], FOURIER_WIDTH),
                    preferred_element_type=F32)
        o_ref[b] = y.astype(BF16).reshape(o_ref.shape[1:])


def _fourier_ab(xcs, stage_a, stage_b, batch):
    r = DFT_RADIX
    c = FOURIER_WIDTH
    nn = 2 * r * r
    params = pltpu.CompilerParams(
        dimension_semantics=("arbitrary",), vmem_limit_bytes=VMEM_LIMIT)
    fix_mid = pl.BlockSpec((batch, 2, r, None, r, c), lambda d: (0, 0, 0, d, 0, 0))
    fix_out = pl.BlockSpec((batch, 2, None, r, r, c), lambda d: (0, 0, d, 0, 0, 0))
    per_digit = pl.BlockSpec((1, nn, nn), lambda d: (d, 0, 0))
    planes = jax.ShapeDtypeStruct((batch, 2, r, r, r, c), BF16)
    x6 = xcs.reshape(batch, 2, r, r, r, c)
    y = pl.pallas_call(
        _dft_stage_kernel, grid=(r,), in_specs=[fix_mid, per_digit],
        out_specs=fix_mid, out_shape=planes, compiler_params=params,
        name="seq_dft_stage_a")(x6, stage_a)
    z = pl.pallas_call(
        _dft_stage_kernel, grid=(r,), in_specs=[fix_out, per_digit],
        out_specs=fix_out, out_shape=planes, compiler_params=params,
        name="seq_dft_stage_b")(y, stage_b)
    return z


def _bias_tiles_kernel(tab_ref, o_ref):
    h = pl.program_id(0)
    tt = ATTN_TILE
    half = REL_BUCKETS // 2
    max_exact = half // 2
    row = lax.broadcasted_iota(jnp.int32, (tt, tt), 0)
    col = lax.broadcasted_iota(jnp.int32, (tt, tt), 1)
    o_ref[0, 0] = jnp.full((tt, tt), tab_ref[half - 1, h] * LOG2E, F32)
    o_ref[0, 4] = jnp.full((tt, tt), tab_ref[REL_BUCKETS - 1, h] * LOG2E, F32)
    for t in range(1, 4):
        rel = row - col + (t - 2) * tt
        ret = jnp.where(rel > 0, half, 0)
        n = jnp.abs(rel)
        nf = jnp.maximum(n, 1).astype(F32)
        large = max_exact + (jnp.log(nf / max_exact) / math.log(REL_MAX_DIST / max_exact)
                             * (half - max_exact)).astype(jnp.int32)
        large = jnp.minimum(large, half - 1)
        bucket = ret + jnp.where(n < max_exact, n, large)
        lo = half if t > 2 else 0
        hi = half if t < 2 else REL_BUCKETS
        acc = jnp.zeros((tt, tt), F32)
        for bk in range(lo, hi):
            acc = jnp.where(bucket == bk, tab_ref[bk, h], acc)
        o_ref[0, t] = acc * LOG2E


def _bias_tiles(table):
    tt = ATTN_TILE
    return pl.pallas_call(
        _bias_tiles_kernel,
        grid=(DIFF_HEADS,),
        in_specs=[pl.BlockSpec(memory_space=pltpu.SMEM)],
        out_specs=pl.BlockSpec((1, 5, tt, tt), lambda h: (h, 0, 0, 0)),
        out_shape=jax.ShapeDtypeStruct((DIFF_HEADS, 5, tt, tt), F32),
        compiler_params=pltpu.CompilerParams(
            dimension_semantics=("arbitrary",), vmem_limit_bytes=VMEM_LIMIT),
        name="rel_bias_tiles",
    )(table)


def _attn_kernel(tab_ref, q_ref, k_ref, vt_ref, bias_ref, lam_ref, g_ref, o_ref, *,
                 lam_init, n_kv):
    h = pl.program_id(0)
    i = pl.program_id(2)
    tt = ATTN_TILE
    q = q_ref[...].astype(F32)
    lane = lax.broadcasted_iota(jnp.int32, q.shape, 1)
    qm = [jnp.where(lane < DIFF_QK_DIM, q, 0.0).astype(BF16),
          jnp.where(lane >= DIFF_QK_DIM, q, 0.0).astype(BF16)]
    c_after = tab_ref[REL_BUCKETS - 1, h] * LOG2E
    c_before = tab_ref[REL_BUCKETS // 2 - 1, h] * LOG2E

    def scores(d, mp):
        j = lax.rem(i + d, n_kv)
        kt = k_ref[pl.ds(pl.multiple_of(j * tt, tt), tt), :]
        s = lax.dot_general(kt, qm[mp], (((1,), (1,)), ((), ())),
                            preferred_element_type=F32)
        if 2 <= d <= n_kv - 2:
            return s, jnp.where(i + d < n_kv, c_after, c_before)
        return s + bias_ref[0, jnp.clip(j - i, -2, 2) + 2], None

    m = [None, None]
    acc = [None, None]
    units = [(d, mp) for d in range(n_kv) for mp in range(2)]
    ahead = [scores(*units[v]) for v in range(ATTN_LOOKAHEAD)]
    for u, (d, mp) in enumerate(units):
        s, const = ahead.pop(0)
        if u + ATTN_LOOKAHEAD < len(units):
            ahead.append(scores(*units[u + ATTN_LOOKAHEAD]))
        vt = vt_ref[0, 0, lax.rem(i + d, n_kv)]
        smax = jnp.max(s, axis=0, keepdims=True)
        if const is not None:
            smax = smax + const
        m_new = smax if m[mp] is None else jnp.maximum(m[mp], smax)
        shift = m_new if const is None else m_new - const
        pv = jnp.dot(vt, jnp.exp2(s - shift).astype(BF16), preferred_element_type=F32)
        acc[mp] = pv if m[mp] is None else jnp.exp2(m[mp] - m_new) * acc[mp] + pv
        m[mp] = m_new
    num = [a[:DIFF_V_DIM] / a[DIFF_V_DIM:DIFF_V_DIM + 1] for a in acc]
    lf = lam_ref[...]
    lam = (jnp.exp(jnp.sum(lf[0:1] * lf[1:2], axis=-1, keepdims=True))
           - jnp.exp(jnp.sum(lf[2:3] * lf[3:4], axis=-1, keepdims=True)) + lam_init)
    o = num[0] - lam * num[1]
    y = (o * lax.rsqrt(jnp.mean(o * o, axis=0, keepdims=True) + RMS_EPS) * g_ref[...]
         * (1.0 - lam_init))
    o_ref[...] = y.T.astype(BF16)


def _attention(q, k, vt, bias, table, lam, subln_g, lam_init, batch, seq):
    tt = ATTN_TILE
    nq = seq // tt
    m = batch * seq
    kern = functools.partial(_attn_kernel, lam_init=lam_init, n_kv=nq)
    return pl.pallas_call(
        kern,
        grid=(DIFF_HEADS, batch, nq),
        in_specs=[
            pl.BlockSpec(memory_space=pltpu.SMEM),
            pl.BlockSpec((tt, DIFF_V_DIM), lambda h, b, i: (b * nq + i, h)),
            pl.BlockSpec((seq, DIFF_V_DIM), lambda h, b, i: (b, h)),
            pl.BlockSpec((1, 1, nq, VT_ROWS, tt), lambda h, b, i: (b, h, 0, 0, 0)),
            pl.BlockSpec((1, 5, tt, tt), lambda h, b, i: (h, 0, 0, 0)),
            _const_spec((4, DIFF_QK_DIM)),
            _const_spec((DIFF_V_DIM, 1)),
        ],
        out_specs=pl.BlockSpec((tt, DIFF_V_DIM), lambda h, b, i: (b * nq + i, h)),
        out_shape=jax.ShapeDtypeStruct((m, DIFF_WIDTH), BF16),
        compiler_params=pltpu.CompilerParams(
            dimension_semantics=("arbitrary", "arbitrary", "arbitrary"),
            vmem_limit_bytes=VMEM_LIMIT),
        name="diff_attention",
    )(table, q, k, vt, bias, lam, subln_g)


def _swiglu_residual(x, g_ref, w1_ref, w3_ref, w2_ref):
    hn = _rms(x, g_ref[...]).astype(BF16)
    acc = x
    for c in range(D_FF // FF_CHUNK):
        cols = slice(c * FF_CHUNK, (c + 1) * FF_CHUNK)
        a = jnp.dot(hn, w1_ref[:, cols], preferred_element_type=F32)
        b = jnp.dot(hn, w3_ref[:, cols], preferred_element_type=F32)
        u = (a * jax.nn.sigmoid(a) * b).astype(BF16)
        acc = acc + jnp.dot(u, w2_ref[cols, :], preferred_element_type=F32)
    return acc


def _ffn_kernel(x_ref, g_ref, w1_ref, w3_ref, w2_ref, gf_ref, o_ref, *, final_norm):
    acc = _swiglu_residual(x_ref[...], g_ref, w1_ref, w3_ref, w2_ref)
    if final_norm:
        acc = _rms(acc, gf_ref[...])
    o_ref[...] = acc


def _ffn(x, g, w1, w3, w2, gf, final_norm):
    m = x.shape[0]
    tm = ROW_TILE
    return pl.pallas_call(
        functools.partial(_ffn_kernel, final_norm=final_norm),
        grid=(m // tm,),
        in_specs=[
            pl.BlockSpec((tm, D_MODEL), lambda i: (i, 0)),
            _const_spec((1, D_MODEL)),
            _const_spec((D_MODEL, D_FF)),
            _const_spec((D_MODEL, D_FF)),
            _const_spec((D_FF, D_MODEL)),
            _const_spec((1, D_MODEL)),
        ],
        out_specs=pl.BlockSpec((tm, D_MODEL), lambda i: (i, 0)),
        out_shape=jax.ShapeDtypeStruct((m, D_MODEL), F32),
        compiler_params=pltpu.CompilerParams(
            dimension_semantics=("arbitrary",), vmem_limit_bytes=VMEM_LIMIT),
        name="swiglu_ffn",
    )(x, g, w1, w3, w2, gf)


def _even_tail_kernel(z_ref, x_ref, a_ref, sc_ref, wo_ref, g_ref, w1_ref, w3_ref, w2_ref,
                      o_ref):
    nb = z_ref.shape[0]
    rows = nb * DFT_RADIX * DFT_RADIX
    f = jnp.concatenate(
        [jnp.dot(sc_ref[...], z_ref[b].reshape(sc_ref.shape[1], FOURIER_WIDTH),
                 preferred_element_type=F32).astype(BF16) for b in range(nb)], axis=0)
    x = x_ref[...].reshape(rows, D_MODEL)
    mix = jnp.concatenate([f, a_ref[...].reshape(rows, DIFF_WIDTH)], axis=1)
    h = x + jnp.dot(mix, wo_ref[...], preferred_element_type=F32)
    o_ref[...] = _swiglu_residual(h, g_ref, w1_ref, w3_ref, w2_ref).reshape(o_ref.shape)


def _even_tail(z, x, att, stage_c, w_out, g, w1, w3, w2, batch):
    r = DFT_RADIX
    nb = EVEN_TAIL_BATCHES
    c = FOURIER_WIDTH

    def tiles(width):
        return pl.BlockSpec((nb, r, None, r, width), lambda d, b: (b, 0, d, 0, 0))

    out = pl.pallas_call(
        _even_tail_kernel,
        grid=(r, batch // nb),
        in_specs=[
            pl.BlockSpec((nb, 2, r, None, r, c), lambda d, b: (b, 0, 0, d, 0, 0)),
            tiles(D_MODEL),
            tiles(DIFF_WIDTH),
            _const_spec((r * r, 2 * r * r)),
            _const_spec((FOURIER_WIDTH + DIFF_WIDTH, D_MODEL)),
            _const_spec((1, D_MODEL)),
            _const_spec((D_MODEL, D_FF)),
            _const_spec((D_MODEL, D_FF)),
            _const_spec((D_FF, D_MODEL)),
        ],
        out_specs=tiles(D_MODEL),
        out_shape=jax.ShapeDtypeStruct((batch, r, r, r, D_MODEL), F32),
        compiler_params=pltpu.CompilerParams(
            dimension_semantics=("arbitrary", "arbitrary"), vmem_limit_bytes=VMEM_LIMIT),
        name="even_tail",
    )(z, x.reshape(batch, r, r, r, D_MODEL), att.reshape(batch, r, r, r, DIFF_WIDTH),
      stage_c, w_out, g, w1, w3, w2)
    return out.reshape(batch * SEQ_LEN, D_MODEL)


def _odd_kernel(x_ref, g_ref, wuv_ref, gv_ref, ws_ref, bs_ref, wo_ref, o_ref, y_ref):
    x = x_ref[...]
    hn = _rms(x, g_ref[...]).astype(BF16)
    z = jnp.dot(hn, wuv_ref[...], preferred_element_type=F32)
    z = 0.5 * z * (1.0 + lax.erf(z * math.sqrt(0.5)))
    u = z[:, :SGU_WIDTH]
    vn = _rms(z[:, SGU_WIDTH:], gv_ref[...]).astype(BF16)
    gd = SGU_WIDTH // SGU_GROUPS
    for r in range(x.shape[0] // SGU_CHUNK):
        rows = slice(r * SGU_CHUNK, (r + 1) * SGU_CHUNK)
        for g in range(SGU_GROUPS):
            cols = slice(g * gd, (g + 1) * gd)
            sv = jnp.dot(ws_ref[g], vn[rows, cols], preferred_element_type=F32) + bs_ref[g]
            y_ref[rows, cols] = (u[rows, cols] * sv).astype(BF16)
    o_ref[...] = x + jnp.dot(y_ref[...], wo_ref[...], preferred_element_type=F32)


def _odd_mixer(x, g, wuv, gv, ws, bs, wo):
    m = x.shape[0]
    tm = ROW_TILE
    gd = SGU_WIDTH // SGU_GROUPS
    return pl.pallas_call(
        _odd_kernel,
        grid=(m // tm,),
        in_specs=[
            pl.BlockSpec((tm, D_MODEL), lambda i: (i, 0)),
            _const_spec((1, D_MODEL)),
            _const_spec((D_MODEL, 2 * SGU_WIDTH)),
            _const_spec((1, SGU_WIDTH)),
            _const_spec((SGU_GROUPS, SGU_CHUNK, SGU_CHUNK)),
            _const_spec((SGU_GROUPS, SGU_CHUNK, gd)),
            _const_spec((SGU_WIDTH, D_MODEL)),
        ],
        out_specs=pl.BlockSpec((tm, D_MODEL), lambda i: (i, 0)),
        out_shape=jax.ShapeDtypeStruct((m, D_MODEL), F32),
        scratch_shapes=[pltpu.VMEM((tm, SGU_WIDTH), BF16)],
        compiler_params=pltpu.CompilerParams(
            dimension_semantics=("arbitrary",), vmem_limit_bytes=VMEM_LIMIT),
        name="sgu_mixer",
    )(x, g, wuv, gv, ws, bs, wo)


def kernel(x, rel_bias_table, norm_mix_g, norm_ffn_g, even_w_in, even_w_out, diff_lambda,
           diff_subln_g, odd_w_uv, odd_v_norm_g, odd_w_s, odd_b_s, odd_w_out, ffn_w1, ffn_w3,
           ffn_w2, final_norm_g):
    batch, seq, d = x.shape
    assert (seq, d) == (SEQ_LEN, D_MODEL)
    depth = norm_mix_g.shape[0]
    assert depth % 2 == 0
    m = batch * seq
    h = x.reshape(m, d)
    stage_a, stage_b, stage_c, cdft = _dft_constants()

    for i in range(depth):
        jx = i // 2
        g_mix = norm_mix_g[i].reshape(1, d)
        if i % 2 == 0:
            lam_init = 0.8 - 0.6 * math.exp(-0.3 * i)
            wv_t = even_w_in[jx][:, FOURIER_WIDTH + 2 * DIFF_WIDTH:].T.astype(BF16)
            xcs, q, k, vt = _even_in(h, g_mix, even_w_in[jx].astype(BF16), wv_t, cdft,
                                     batch, seq)
            z = _fourier_ab(xcs, stage_a, stage_b, batch)
            bias = _bias_tiles(rel_bias_table)
            att = _attention(q, k, vt, bias, rel_bias_table, diff_lambda[jx],
                             diff_subln_g[jx].reshape(DIFF_V_DIM, 1), lam_init, batch, seq)
            h = _even_tail(z, h, att, stage_c, even_w_out[jx].astype(BF16),
                           norm_ffn_g[i].reshape(1, d), ffn_w1[i].astype(BF16),
                           ffn_w3[i].astype(BF16), ffn_w2[i].astype(BF16), batch)
            continue
        else:
            bs = jnp.broadcast_to(odd_b_s[jx][:, :, None],
                                  (SGU_GROUPS, SGU_CHUNK, SGU_WIDTH // SGU_GROUPS))
            h = _odd_mixer(h, g_mix, odd_w_uv[jx].astype(BF16),
                           odd_v_norm_g[jx].reshape(1, SGU_WIDTH),
                           odd_w_s[jx].astype(BF16), bs, odd_w_out[jx].astype(BF16))
        h = _ffn(h, norm_ffn_g[i].reshape(1, d),
                 ffn_w1[i].astype(BF16), ffn_w3[i].astype(BF16), ffn_w2[i].astype(BF16),
                 final_norm_g.reshape(1, d), i == depth - 1)
    return h.reshape(batch, seq, d)
```

```python
import functools
import math

import numpy as np
import jax
import jax.numpy as jnp
from jax import lax
from jax.experimental import pallas as pl
from jax.experimental.pallas import tpu as pltpu

F32 = jnp.float32
BF16 = jnp.bfloat16

D_MODEL = 1024
FOURIER_GROUP_DIM = 128
FOURIER_WIDTH = 512
DIFF_HEADS = 4
DIFF_QK_DIM = 64
DIFF_V_DIM = 128
DIFF_WIDTH = 512
REL_BUCKETS = 32
REL_MAX_DIST = 128
SGU_CHUNK = 128
SGU_GROUPS = 8
SGU_WIDTH = 1024
D_FF = 2816
RMS_EPS = 1e-6
LOG2E = math.log2(math.e)

DFT_RADIX = 16
SEQ_LEN = DFT_RADIX ** 3

ROW_TILE = 512
EVEN_TAIL_BATCHES = 2
ATTN_TILE = 512
ATTN_LOOKAHEAD = 2
VT_ROWS = DIFF_V_DIM + 16
FF_CHUNK = 256
VMEM_LIMIT = 56 * 1024 * 1024


def _rms(x, g):
    return x * lax.rsqrt(jnp.mean(x * x, axis=-1, keepdims=True) + RMS_EPS) * g


def _const_spec(shape):
    return pl.BlockSpec(shape, lambda *_: (0,) * len(shape))


def _dft_constants():
    r = DFT_RADIX
    n = SEQ_LEN
    i = np.arange(r)
    f = np.exp(-2j * np.pi * np.outer(i, i) / r)
    eye = np.eye(r)

    def real_rep(g):
        return np.block([[g.real, -g.imag], [g.imag, g.real]])

    stage_a, stage_b = [], []
    for t in range(r):
        tw = np.exp(-2j * np.pi * t * i / (r * r))
        g = np.einsum('pr,kn,k->pknr', eye, f, tw).reshape(r * r, r * r)
        stage_a.append(real_rep(g))
    for rr in range(r):
        tw = np.exp(-2j * np.pi * rr * (i[None, :] + r * i[:, None]) / n)
        g = np.einsum('qk,jt,jq->jqtk', eye, f, tw).reshape(r * r, r * r)
        stage_b.append(real_rep(g))
    scale = 1.0 / math.sqrt(n * FOURIER_GROUP_DIM)
    g = np.einsum('qk,jr->jqrk', eye, f).reshape(r * r, r * r) * scale
    stage_c = np.concatenate([g.real, -g.imag], axis=1)
    cidx = np.arange(FOURIER_GROUP_DIM)
    cang = 2.0 * np.pi * np.outer(cidx, cidx) / FOURIER_GROUP_DIM
    eye2 = np.eye(2)
    cdft = np.stack([np.kron(eye2, np.cos(cang)), -np.kron(eye2, np.sin(cang))])
    return (jnp.asarray(np.stack(stage_a), BF16), jnp.asarray(np.stack(stage_b), BF16),
            jnp.asarray(stage_c, BF16), jnp.asarray(cdft, BF16))


def _even_in_kernel(x_ref, g_ref, w_ref, wvt_ref, cdft_ref, xcs_ref, q_ref, k_ref, vt_ref):
    hn = _rms(x_ref[...], g_ref[...]).astype(BF16)
    n_fqk = FOURIER_WIDTH + 2 * DIFF_WIDTH
    z = jnp.dot(hn, w_ref[:, :n_fqk], preferred_element_type=F32)
    vt = lax.dot_general(wvt_ref[...], hn, (((1,), (1,)), ((), ())),
                         preferred_element_type=F32)
    pad_rows = lax.broadcasted_iota(jnp.int32, (VT_ROWS - DIFF_V_DIM, vt.shape[1]), 0)
    ones_tile = jnp.where(pad_rows == 0, 1.0, 0.0).astype(BF16)
    for hh in range(DIFF_HEADS):
        vt_ref[0, hh, 0, :DIFF_V_DIM] = vt[hh * DIFF_V_DIM:(hh + 1) * DIFF_V_DIM].astype(BF16)
        vt_ref[0, hh, 0, DIFF_V_DIM:] = ones_tile
    zf = z[:, :FOURIER_WIDTH].astype(BF16)
    for p in range(FOURIER_WIDTH // 256):
        zp = zf[:, p * 256:(p + 1) * 256]
        xcs_ref[0, 0, :, p * 256:(p + 1) * 256] = jnp.dot(
            zp, cdft_ref[0], preferred_element_type=F32).astype(BF16)
        xcs_ref[0, 1, :, p * 256:(p + 1) * 256] = jnp.dot(
            zp, cdft_ref[1], preferred_element_type=F32).astype(BF16)
    o = FOURIER_WIDTH
    q_ref[...] = (z[:, o:o + 512] * (DIFF_QK_DIM ** -0.5 * LOG2E)).astype(BF16)
    k_ref[...] = z[:, o + 512:o + 1024].astype(BF16)


def _even_in(x, g, w, wvt, cdft, batch, seq):
    m = x.shape[0]
    tm = ROW_TILE
    assert tm == ATTN_TILE
    per_b = seq // tm
    n_out = w.shape[1]
    return pl.pallas_call(
        _even_in_kernel,
        grid=(m // tm,),
        in_specs=[
            pl.BlockSpec((tm, D_MODEL), lambda i: (i, 0)),
            _const_spec((1, D_MODEL)),
            _const_spec((D_MODEL, n_out)),
            _const_spec((DIFF_WIDTH, D_MODEL)),
            _const_spec((2, 256, 256)),
        ],
        out_specs=[
            pl.BlockSpec((1, 2, tm, FOURIER_WIDTH), lambda i: (i // per_b, 0, i % per_b, 0)),
            pl.BlockSpec((tm, 512), lambda i: (i, 0)),
            pl.BlockSpec((tm, 512), lambda i: (i, 0)),
            pl.BlockSpec((1, DIFF_HEADS, 1, VT_ROWS, tm),
                         lambda i: (i // per_b, 0, i % per_b, 0, 0)),
        ],
        out_shape=[
            jax.ShapeDtypeStruct((batch, 2, seq, FOURIER_WIDTH), BF16),
            jax.ShapeDtypeStruct((m, 512), BF16),
            jax.ShapeDtypeStruct((m, 512), BF16),
            jax.ShapeDtypeStruct((batch, DIFF_HEADS, per_b, VT_ROWS, tm), BF16),
        ],
        compiler_params=pltpu.CompilerParams(
            dimension_semantics=("arbitrary",), vmem_limit_bytes=VMEM_LIMIT),
        name="even_in_proj",
    )(x, g, w, wvt, cdft)


def _dft_stage_kernel(x_ref, w_ref, o_ref):
    w = w_ref[0] if len(w_ref.shape) == 3 else w_ref[...]
    for b in range(x_ref.shape[0]):
        y = jnp.dot(w, x_ref[b].reshape(w.shape[1], FOURIER_WIDTH),
                    preferred_element_type=F32)
        o_ref[b] = y.astype(BF16).reshape(o_ref.shape[1:])


def _fourier_ab(xcs, stage_a, stage_b, batch):
    r = DFT_RADIX
    c = FOURIER_WIDTH
    nn = 2 * r * r
    params = pltpu.CompilerParams(
        dimension_semantics=("arbitrary",), vmem_limit_bytes=VMEM_LIMIT)
    fix_mid = pl.BlockSpec((batch, 2, r, None, r, c), lambda d: (0, 0, 0, d, 0, 0))
    fix_out = pl.BlockSpec((batch, 2, None, r, r, c), lambda d: (0, 0, d, 0, 0, 0))
    per_digit = pl.BlockSpec((1, nn, nn), lambda d: (d, 0, 0))
    planes = jax.ShapeDtypeStruct((batch, 2, r, r, r, c), BF16)
    x6 = xcs.reshape(batch, 2, r, r, r, c)
    y = pl.pallas_call(
        _dft_stage_kernel, grid=(r,), in_specs=[fix_mid, per_digit],
        out_specs=fix_mid, out_shape=planes, compiler_params=params,
        name="seq_dft_stage_a")(x6, stage_a)
    z = pl.pallas_call(
        _dft_stage_kernel, grid=(r,), in_specs=[fix_out, per_digit],
        out_specs=fix_out, out_shape=planes, compiler_params=params,
        name="seq_dft_stage_b")(y, stage_b)
    return z


def _bias_tiles_kernel(tab_ref, o_ref):
    h = pl.program_id(0)
    tt = ATTN_TILE
    half = REL_BUCKETS // 2
    max_exact = half // 2
    row = lax.broadcasted_iota(jnp.int32, (tt, tt), 0)
    col = lax.broadcasted_iota(jnp.int32, (tt, tt), 1)
    o_ref[0, 0] = jnp.full((tt, tt), tab_ref[half - 1, h] * LOG2E, F32)
    o_ref[0, 4] = jnp.full((tt, tt), tab_ref[REL_BUCKETS - 1, h] * LOG2E, F32)
    for t in range(1, 4):
        rel = row - col + (t - 2) * tt
        ret = jnp.where(rel > 0, half, 0)
        n = jnp.abs(rel)
        nf = jnp.maximum(n, 1).astype(F32)
        large = max_exact + (jnp.log(nf / max_exact) / math.log(REL_MAX_DIST / max_exact)
                             * (half - max_exact)).astype(jnp.int32)
        large = jnp.minimum(large, half - 1)
        bucket = ret + jnp.where(n < max_exact, n, large)
        lo = half if t > 2 else 0
        hi = half if t < 2 else REL_BUCKETS
        acc = jnp.zeros((tt, tt), F32)
        for bk in range(lo, hi):
            acc = jnp.where(bucket == bk, tab_ref[bk, h], acc)
        o_ref[0, t] = acc * LOG2E


def _bias_tiles(table):
    tt = ATTN_TILE
    return pl.pallas_call(
        _bias_tiles_kernel,
        grid=(DIFF_HEADS,),
        in_specs=[pl.BlockSpec(memory_space=pltpu.SMEM)],
        out_specs=pl.BlockSpec((1, 5, tt, tt), lambda h: (h, 0, 0, 0)),
        out_shape=jax.ShapeDtypeStruct((DIFF_HEADS, 5, tt, tt), F32),
        compiler_params=pltpu.CompilerParams(
            dimension_semantics=("arbitrary",), vmem_limit_bytes=VMEM_LIMIT),
        name="rel_bias_tiles",
    )(table)


def _attn_kernel(tab_ref, q_ref, k_ref, vt_ref, bias_ref, lam_ref, g_ref, o_ref, *,
                 lam_init, n_kv):
    h = pl.program_id(0)
    i = pl.program_id(2)
    tt = ATTN_TILE
    qt = q_ref[...].astype(F32).T
    row = lax.broadcasted_iota(jnp.int32, qt.shape, 0)
    qm = [jnp.where(row < DIFF_QK_DIM, qt, 0.0).astype(BF16),
          jnp.where(row >= DIFF_QK_DIM, qt, 0.0).astype(BF16)]
    c_after = tab_ref[REL_BUCKETS - 1, h] * LOG2E
    c_before = tab_ref[REL_BUCKETS // 2 - 1, h] * LOG2E

    def scores(d, mp):
        j = lax.rem(i + d, n_kv)
        kt = k_ref[pl.ds(pl.multiple_of(j * tt, tt), tt), :]
        s = jnp.dot(kt, qm[mp], preferred_element_type=F32)
        if 2 <= d <= n_kv - 2:
            return s, jnp.where(i + d < n_kv, c_after, c_before)
        return s + bias_ref[0, jnp.clip(j - i, -2, 2) + 2], None

    m = [None, None]
    acc = [None, None]
    units = [(d, mp) for d in range(n_kv) for mp in range(2)]
    ahead = [scores(*units[v]) for v in range(ATTN_LOOKAHEAD)]
    for u, (d, mp) in enumerate(units):
        s, const = ahead.pop(0)
        if u + ATTN_LOOKAHEAD < len(units):
            ahead.append(scores(*units[u + ATTN_LOOKAHEAD]))
        vt = vt_ref[0, 0, lax.rem(i + d, n_kv)]
        smax = jnp.max(s, axis=0, keepdims=True)
        if const is not None:
            smax = smax + const
        m_new = smax if m[mp] is None else jnp.maximum(m[mp], smax)
        shift = m_new if const is None else m_new - const
        pv = jnp.dot(vt, jnp.exp2(s - shift).astype(BF16), preferred_element_type=F32)
        acc[mp] = pv if m[mp] is None else jnp.exp2(m[mp] - m_new) * acc[mp] + pv
        m[mp] = m_new
    num = [a[:DIFF_V_DIM] / a[DIFF_V_DIM:DIFF_V_DIM + 1] for a in acc]
    lf = lam_ref[...]
    lam = (jnp.exp(jnp.sum(lf[0:1] * lf[1:2], axis=-1, keepdims=True))
           - jnp.exp(jnp.sum(lf[2:3] * lf[3:4], axis=-1, keepdims=True)) + lam_init)
    o = num[0] - lam * num[1]
    y = (o * lax.rsqrt(jnp.mean(o * o, axis=0, keepdims=True) + RMS_EPS) * g_ref[...]
         * (1.0 - lam_init))
    o_ref[...] = y.T.astype(BF16)


def _attention(q, k, vt, bias, table, lam, subln_g, lam_init, batch, seq):
    tt = ATTN_TILE
    nq = seq // tt
    m = batch * seq
    kern = functools.partial(_attn_kernel, lam_init=lam_init, n_kv=nq)
    return pl.pallas_call(
        kern,
        grid=(DIFF_HEADS, batch, nq),
        in_specs=[
            pl.BlockSpec(memory_space=pltpu.SMEM),
            pl.BlockSpec((tt, DIFF_V_DIM), lambda h, b, i: (b * nq + i, h)),
            pl.BlockSpec((seq, DIFF_V_DIM), lambda h, b, i: (b, h)),
            pl.BlockSpec((1, 1, nq, VT_ROWS, tt), lambda h, b, i: (b, h, 0, 0, 0)),
            pl.BlockSpec((1, 5, tt, tt), lambda h, b, i: (h, 0, 0, 0)),
            _const_spec((4, DIFF_QK_DIM)),
            _const_spec((DIFF_V_DIM, 1)),
        ],
        out_specs=pl.BlockSpec((tt, DIFF_V_DIM), lambda h, b, i: (b * nq + i, h)),
        out_shape=jax.ShapeDtypeStruct((m, DIFF_WIDTH), BF16),
        compiler_params=pltpu.CompilerParams(
            dimension_semantics=("arbitrary", "arbitrary", "arbitrary"),
            vmem_limit_bytes=VMEM_LIMIT),
        name="diff_attention",
    )(table, q, k, vt, bias, lam, subln_g)


def _swiglu_residual(x, g_ref, w1_ref, w3_ref, w2_ref):
    hn = _rms(x, g_ref[...]).astype(BF16)
    acc = x
    for c in range(D_FF // FF_CHUNK):
        cols = slice(c * FF_CHUNK, (c + 1) * FF_CHUNK)
        a = jnp.dot(hn, w1_ref[:, cols], preferred_element_type=F32)
        b = jnp.dot(hn, w3_ref[:, cols], preferred_element_type=F32)
        u = (a * jax.nn.sigmoid(a) * b).astype(BF16)
        acc = acc + jnp.dot(u, w2_ref[cols, :], preferred_element_type=F32)
    return acc


def _ffn_kernel(x_ref, g_ref, w1_ref, w3_ref, w2_ref, gf_ref, o_ref, *, final_norm):
    acc = _swiglu_residual(x_ref[...], g_ref, w1_ref, w3_ref, w2_ref)
    if final_norm:
        acc = _rms(acc, gf_ref[...])
    o_ref[...] = acc


def _ffn(x, g, w1, w3, w2, gf, final_norm):
    m = x.shape[0]
    tm = ROW_TILE
    return pl.pallas_call(
        functools.partial(_ffn_kernel, final_norm=final_norm),
        grid=(m // tm,),
        in_specs=[
            pl.BlockSpec((tm, D_MODEL), lambda i: (i, 0)),
            _const_spec((1, D_MODEL)),
            _const_spec((D_MODEL, D_FF)),
            _const_spec((D_MODEL, D_FF)),
            _const_spec((D_FF, D_MODEL)),
            _const_spec((1, D_MODEL)),
        ],
        out_specs=pl.BlockSpec((tm, D_MODEL), lambda i: (i, 0)),
        out_shape=jax.ShapeDtypeStruct((m, D_MODEL), F32),
        compiler_params=pltpu.CompilerParams(
            dimension_semantics=("arbitrary",), vmem_limit_bytes=VMEM_LIMIT),
        name="swiglu_ffn",
    )(x, g, w1, w3, w2, gf)


def _even_tail_kernel(z_ref, x_ref, a_ref, sc_ref, wo_ref, g_ref, w1_ref, w3_ref, w2_ref,
                      o_ref):
    nb = z_ref.shape[0]
    rows = nb * DFT_RADIX * DFT_RADIX
    f = jnp.concatenate(
        [jnp.dot(sc_ref[...], z_ref[b].reshape(sc_ref.shape[1], FOURIER_WIDTH),
                 preferred_element_type=F32).astype(BF16) for b in range(nb)], axis=0)
    x = x_ref[...].reshape(rows, D_MODEL)
    mix = jnp.concatenate([f, a_ref[...].reshape(rows, DIFF_WIDTH)], axis=1)
    h = x + jnp.dot(mix, wo_ref[...], preferred_element_type=F32)
    o_ref[...] = _swiglu_residual(h, g_ref, w1_ref, w3_ref, w2_ref).reshape(o_ref.shape)


def _even_tail(z, x, att, stage_c, w_out, g, w1, w3, w2, batch):
    r = DFT_RADIX
    nb = EVEN_TAIL_BATCHES
    c = FOURIER_WIDTH

    def tiles(width):
        return pl.BlockSpec((nb, r, None, r, width), lambda d, b: (b, 0, d, 0, 0))

    out = pl.pallas_call(
        _even_tail_kernel,
        grid=(r, batch // nb),
        in_specs=[
            pl.BlockSpec((nb, 2, r, None, r, c), lambda d, b: (b, 0, 0, d, 0, 0)),
            tiles(D_MODEL),
            tiles(DIFF_WIDTH),
            _const_spec((r * r, 2 * r * r)),
            _const_spec((FOURIER_WIDTH + DIFF_WIDTH, D_MODEL)),
            _const_spec((1, D_MODEL)),
            _const_spec((D_MODEL, D_FF)),
            _const_spec((D_MODEL, D_FF)),
            _const_spec((D_FF, D_MODEL)),
        ],
        out_specs=tiles(D_MODEL),
        out_shape=jax.ShapeDtypeStruct((batch, r, r, r, D_MODEL), F32),
        compiler_params=pltpu.CompilerParams(
            dimension_semantics=("arbitrary", "arbitrary"), vmem_limit_bytes=VMEM_LIMIT),
        name="even_tail",
    )(z, x.reshape(batch, r, r, r, D_MODEL), att.reshape(batch, r, r, r, DIFF_WIDTH),
      stage_c, w_out, g, w1, w3, w2)
    return out.reshape(batch * SEQ_LEN, D_MODEL)


def _odd_kernel(x_ref, g_ref, wuv_ref, gv_ref, ws_ref, bs_ref, wo_ref, o_ref, y_ref):
    x = x_ref[...]
    hn = _rms(x, g_ref[...]).astype(BF16)
    z = jnp.dot(hn, wuv_ref[...], preferred_element_type=F32)
    z = 0.5 * z * (1.0 + lax.erf(z * math.sqrt(0.5)))
    u = z[:, :SGU_WIDTH]
    vn = _rms(z[:, SGU_WIDTH:], gv_ref[...]).astype(BF16)
    gd = SGU_WIDTH // SGU_GROUPS
    for r in range(x.shape[0] // SGU_CHUNK):
        rows = slice(r * SGU_CHUNK, (r + 1) * SGU_CHUNK)
        for g in range(SGU_GROUPS):
            cols = slice(g * gd, (g + 1) * gd)
            sv = jnp.dot(ws_ref[g], vn[rows, cols], preferred_element_type=F32) + bs_ref[g]
            y_ref[rows, cols] = (u[rows, cols] * sv).astype(BF16)
    o_ref[...] = x + jnp.dot(y_ref[...], wo_ref[...], preferred_element_type=F32)


def _odd_mixer(x, g, wuv, gv, ws, bs, wo):
    m = x.shape[0]
    tm = ROW_TILE
    gd = SGU_WIDTH // SGU_GROUPS
    return pl.pallas_call(
        _odd_kernel,
        grid=(m // tm,),
        in_specs=[
            pl.BlockSpec((tm, D_MODEL), lambda i: (i, 0)),
            _const_spec((1, D_MODEL)),
            _const_spec((D_MODEL, 2 * SGU_WIDTH)),
            _const_spec((1, SGU_WIDTH)),
            _const_spec((SGU_GROUPS, SGU_CHUNK, SGU_CHUNK)),
            _const_spec((SGU_GROUPS, SGU_CHUNK, gd)),
            _const_spec((SGU_WIDTH, D_MODEL)),
        ],
        out_specs=pl.BlockSpec((tm, D_MODEL), lambda i: (i, 0)),
        out_shape=jax.ShapeDtypeStruct((m, D_MODEL), F32),
        scratch_shapes=[pltpu.VMEM((tm, SGU_WIDTH), BF16)],
        compiler_params=pltpu.CompilerParams(
            dimension_semantics=("arbitrary",), vmem_limit_bytes=VMEM_LIMIT),
        name="sgu_mixer",
    )(x, g, wuv, gv, ws, bs, wo)


def kernel(x, rel_bias_table, norm_mix_g, norm_ffn_g, even_w_in, even_w_out, diff_lambda,
           diff_subln_g, odd_w_uv, odd_v_norm_g, odd_w_s, odd_b_s, odd_w_out, ffn_w1, ffn_w3,
           ffn_w2, final_norm_g):
    batch, seq, d = x.shape
    assert (seq, d) == (SEQ_LEN, D_MODEL)
    depth = norm_mix_g.shape[0]
    assert depth % 2 == 0
    m = batch * seq
    h = x.reshape(m, d)
    stage_a, stage_b, stage_c, cdft = _dft_constants()

    for i in range(depth):
        jx = i // 2
        g_mix = norm_mix_g[i].reshape(1, d)
        if i % 2 == 0:
            lam_init = 0.8 - 0.6 * math.exp(-0.3 * i)
            wv_t = even_w_in[jx][:, FOURIER_WIDTH + 2 * DIFF_WIDTH:].T.astype(BF16)
            xcs, q, k, vt = _even_in(h, g_mix, even_w_in[jx].astype(BF16), wv_t, cdft,
                                     batch, seq)
            z = _fourier_ab(xcs, stage_a, stage_b, batch)
            bias = _bias_tiles(rel_bias_table)
            att = _attention(q, k, vt, bias, rel_bias_table, diff_lambda[jx],
                             diff_subln_g[jx].reshape(DIFF_V_DIM, 1), lam_init, batch, seq)
            h = _even_tail(z, h, att, stage_c, even_w_out[jx].astype(BF16),
                           norm_ffn_g[i].reshape(1, d), ffn_w1[i].astype(BF16),
                           ffn_w3[i].astype(BF16), ffn_w2[i].astype(BF16), batch)
            continue
        else:
            bs = jnp.broadcast_to(odd_b_s[jx][:, :, None],
                                  (SGU_GROUPS, SGU_CHUNK, SGU_WIDTH // SGU_GROUPS))
            h = _odd_mixer(h, g_mix, odd_w_uv[jx].astype(BF16),
                           odd_v_norm_g[jx].reshape(1, SGU_WIDTH),
                           odd_w_s[jx].astype(BF16), bs, odd_w_out[jx].astype(BF16))
        h = _ffn(h, norm_ffn_g[i].reshape(1, d),
                 ffn_w1[i].astype(BF16), ffn_w3[i].astype(BF16), ffn_w2[i].astype(BF16),
                 final_norm_g.reshape(1, d), i == depth - 1)
    return h.reshape(batch, seq, d)
```

```python
import functools
import math

import numpy as np
import jax
import jax.numpy as jnp
from jax import lax
from jax.experimental import pallas as pl
from jax.experimental.pallas import tpu as pltpu

F32 = jnp.float32
BF16 = jnp.bfloat16

D_MODEL = 1024
FOURIER_GROUP_DIM = 128
FOURIER_WIDTH = 512
DIFF_HEADS = 4
DIFF_QK_DIM = 64
DIFF_V_DIM = 128
DIFF_WIDTH = 512
REL_BUCKETS = 32
REL_MAX_DIST = 128
SGU_CHUNK = 128
SGU_GROUPS = 8
SGU_WIDTH = 1024
D_FF = 2816
RMS_EPS = 1e-6
LOG2E = math.log2(math.e)

DFT_RADIX = 16
SEQ_LEN = DFT_RADIX ** 3

ROW_TILE = 512
SGU_SUBTILES = 2
EVEN_TAIL_BATCHES = 2
ATTN_TILE = 512
ATTN_LOOKAHEAD = 2
VT_ROWS = DIFF_V_DIM + 16
FF_CHUNK = 256
CAST_STEPS = 8
VMEM_LIMIT = 56 * 1024 * 1024


def _rms(x, g):
    return x * lax.rsqrt(jnp.mean(x * x, axis=-1, keepdims=True) + RMS_EPS) * g


def _const_spec(shape):
    return pl.BlockSpec(shape, lambda *_: (0,) * len(shape))


def _dft_constants():
    r = DFT_RADIX
    n = SEQ_LEN
    i = np.arange(r)
    f = np.exp(-2j * np.pi * np.outer(i, i) / r)
    eye = np.eye(r)

    def real_rep(g):
        return np.block([[g.real, -g.imag], [g.imag, g.real]])

    stage_a, stage_b = [], []
    for t in range(r):
        tw = np.exp(-2j * np.pi * t * i / (r * r))
        g = np.einsum('pr,kn,k->pknr', eye, f, tw).reshape(r * r, r * r)
        stage_a.append(real_rep(g))
    for rr in range(r):
        tw = np.exp(-2j * np.pi * rr * (i[None, :] + r * i[:, None]) / n)
        g = np.einsum('qk,jt,jq->jqtk', eye, f, tw).reshape(r * r, r * r)
        stage_b.append(real_rep(g))
    scale = 1.0 / math.sqrt(n * FOURIER_GROUP_DIM)
    g = np.einsum('qk,jr->jqrk', eye, f).reshape(r * r, r * r) * scale
    stage_c = np.concatenate([g.real, -g.imag], axis=1)
    cidx = np.arange(FOURIER_GROUP_DIM)
    cang = 2.0 * np.pi * np.outer(cidx, cidx) / FOURIER_GROUP_DIM
    eye2 = np.eye(2)
    cdft = np.stack([np.kron(eye2, np.cos(cang)), -np.kron(eye2, np.sin(cang))])
    return (jnp.asarray(np.stack(stage_a), BF16), jnp.asarray(np.stack(stage_b), BF16),
            jnp.asarray(stage_c, BF16), jnp.asarray(cdft, BF16))


def _even_in_kernel(x_ref, g_ref, w_ref, wvt_ref, cdft_ref, xcs_ref, q_ref, k_ref, vt_ref):
    hn = _rms(x_ref[...], g_ref[...]).astype(BF16)
    n_fqk = FOURIER_WIDTH + 2 * DIFF_WIDTH
    z = jnp.dot(hn, w_ref[:, :n_fqk], preferred_element_type=F32)
    vt = lax.dot_general(wvt_ref[...], hn, (((1,), (1,)), ((), ())),
                         preferred_element_type=F32)
    pad_rows = lax.broadcasted_iota(jnp.int32, (VT_ROWS - DIFF_V_DIM, vt.shape[1]), 0)
    ones_tile = jnp.where(pad_rows == 0, 1.0, 0.0).astype(BF16)
    for hh in range(DIFF_HEADS):
        vt_ref[0, hh, 0, :DIFF_V_DIM] = vt[hh * DIFF_V_DIM:(hh + 1) * DIFF_V_DIM].astype(BF16)
        vt_ref[0, hh, 0, DIFF_V_DIM:] = ones_tile
    zf = z[:, :FOURIER_WIDTH].astype(BF16)
    for p in range(FOURIER_WIDTH // 256):
        zp = zf[:, p * 256:(p + 1) * 256]
        xcs_ref[0, 0, :, p * 256:(p + 1) * 256] = jnp.dot(
            zp, cdft_ref[0], preferred_element_type=F32).astype(BF16)
        xcs_ref[0, 1, :, p * 256:(p + 1) * 256] = jnp.dot(
            zp, cdft_ref[1], preferred_element_type=F32).astype(BF16)
    o = FOURIER_WIDTH
    q_ref[...] = (z[:, o:o + 512] * (DIFF_QK_DIM ** -0.5 * LOG2E)).astype(BF16)
    k_ref[...] = z[:, o + 512:o + 1024].astype(BF16)


def _even_in(x, g, w, wvt, cdft, batch, seq):
    m = x.shape[0]
    tm = ROW_TILE
    assert tm == ATTN_TILE
    per_b = seq // tm
    n_out = w.shape[1]
    return pl.pallas_call(
        _even_in_kernel,
        grid=(m // tm,),
        in_specs=[
            pl.BlockSpec((tm, D_MODEL), lambda i: (i, 0)),
            _const_spec((1, D_MODEL)),
            _const_spec((D_MODEL, n_out)),
            _const_spec((DIFF_WIDTH, D_MODEL)),
            _const_spec((2, 256, 256)),
        ],
        out_specs=[
            pl.BlockSpec((1, 2, tm, FOURIER_WIDTH), lambda i: (i // per_b, 0, i % per_b, 0)),
            pl.BlockSpec((tm, 512), lambda i: (i, 0)),
            pl.BlockSpec((tm, 512), lambda i: (i, 0)),
            pl.BlockSpec((1, DIFF_HEADS, 1, VT_ROWS, tm),
                         lambda i: (i // per_b, 0, i % per_b, 0, 0)),
        ],
        out_shape=[
            jax.ShapeDtypeStruct((batch, 2, seq, FOURIER_WIDTH), BF16),
            jax.ShapeDtypeStruct((m, 512), BF16),
            jax.ShapeDtypeStruct((m, 512), BF16),
            jax.ShapeDtypeStruct((batch, DIFF_HEADS, per_b, VT_ROWS, tm), BF16),
        ],
        compiler_params=pltpu.CompilerParams(
            dimension_semantics=("arbitrary",), vmem_limit_bytes=VMEM_LIMIT),
        name="even_in_proj",
    )(x, g, w, wvt, cdft)


def _dft_stage_kernel(x_ref, w_ref, o_ref):
    w = w_ref[0] if len(w_ref.shape) == 3 else w_ref[...]
    for b in range(x_ref.shape[0]):
        y = jnp.dot(w, x_ref[b].reshape(w.shape[1], FOURIER_WIDTH),
                    preferred_element_type=F32)
        o_ref[b] = y.astype(BF16).reshape(o_ref.shape[1:])


def _fourier_ab(xcs, stage_a, stage_b, batch):
    r = DFT_RADIX
    c = FOURIER_WIDTH
    nn = 2 * r * r
    params = pltpu.CompilerParams(
        dimension_semantics=("arbitrary",), vmem_limit_bytes=VMEM_LIMIT)
    fix_mid = pl.BlockSpec((batch, 2, r, None, r, c), lambda d: (0, 0, 0, d, 0, 0))
    fix_out = pl.BlockSpec((batch, 2, None, r, r, c), lambda d: (0, 0, d, 0, 0, 0))
    per_digit = pl.BlockSpec((1, nn, nn), lambda d: (d, 0, 0))
    planes = jax.ShapeDtypeStruct((batch, 2, r, r, r, c), BF16)
    x6 = xcs.reshape(batch, 2, r, r, r, c)
    y = pl.pallas_call(
        _dft_stage_kernel, grid=(r,), in_specs=[fix_mid, per_digit],
        out_specs=fix_mid, out_shape=planes, compiler_params=params,
        name="seq_dft_stage_a")(x6, stage_a)
    z = pl.pallas_call(
        _dft_stage_kernel, grid=(r,), in_specs=[fix_out, per_digit],
        out_specs=fix_out, out_shape=planes, compiler_params=params,
        name="seq_dft_stage_b")(y, stage_b)
    return z


def _bias_tiles_kernel(tab_ref, o_ref):
    h = pl.program_id(0)
    tt = ATTN_TILE
    half = REL_BUCKETS // 2
    max_exact = half // 2
    row = lax.broadcasted_iota(jnp.int32, (tt, tt), 0)
    col = lax.broadcasted_iota(jnp.int32, (tt, tt), 1)
    o_ref[0, 0] = jnp.full((tt, tt), tab_ref[half - 1, h] * LOG2E, F32)
    o_ref[0, 4] = jnp.full((tt, tt), tab_ref[REL_BUCKETS - 1, h] * LOG2E, F32)
    for t in range(1, 4):
        rel = row - col + (t - 2) * tt
        ret = jnp.where(rel > 0, half, 0)
        n = jnp.abs(rel)
        nf = jnp.maximum(n, 1).astype(F32)
        large = max_exact + (jnp.log(nf / max_exact) / math.log(REL_MAX_DIST / max_exact)
                             * (half - max_exact)).astype(jnp.int32)
        large = jnp.minimum(large, half - 1)
        bucket = ret + jnp.where(n < max_exact, n, large)
        lo = half if t > 2 else 0
        hi = half if t < 2 else REL_BUCKETS
        acc = jnp.zeros((tt, tt), F32)
        for bk in range(lo, hi):
            acc = jnp.where(bucket == bk, tab_ref[bk, h], acc)
        o_ref[0, t] = acc * LOG2E


def _bias_tiles(table):
    tt = ATTN_TILE
    return pl.pallas_call(
        _bias_tiles_kernel,
        grid=(DIFF_HEADS,),
        in_specs=[pl.BlockSpec(memory_space=pltpu.SMEM)],
        out_specs=pl.BlockSpec((1, 5, tt, tt), lambda h: (h, 0, 0, 0)),
        out_shape=jax.ShapeDtypeStruct((DIFF_HEADS, 5, tt, tt), F32),
        compiler_params=pltpu.CompilerParams(
            dimension_semantics=("arbitrary",), vmem_limit_bytes=VMEM_LIMIT),
        name="rel_bias_tiles",
    )(table)


def _attn_kernel(tab_ref, q_ref, k_ref, vt_ref, bias_ref, lam_ref, g_ref, o_ref, *,
                 lam_init, n_kv):
    h = pl.program_id(0)
    i = pl.program_id(2)
    tt = ATTN_TILE
    qt = q_ref[...].astype(F32).T
    row = lax.broadcasted_iota(jnp.int32, qt.shape, 0)
    qm = [jnp.where(row < DIFF_QK_DIM, qt, 0.0).astype(BF16),
          jnp.where(row >= DIFF_QK_DIM, qt, 0.0).astype(BF16)]
    c_after = tab_ref[REL_BUCKETS - 1, h] * LOG2E
    c_before = tab_ref[REL_BUCKETS // 2 - 1, h] * LOG2E

    def scores(d, mp):
        j = lax.rem(i + d, n_kv)
        kt = k_ref[pl.ds(pl.multiple_of(j * tt, tt), tt), :]
        s = jnp.dot(kt, qm[mp], preferred_element_type=F32)
        if 2 <= d <= n_kv - 2:
            return s, jnp.where(i + d < n_kv, c_after, c_before)
        return s + bias_ref[0, jnp.clip(j - i, -2, 2) + 2], None

    m = [None, None]
    acc = [None, None]
    units = [(d, mp) for d in range(n_kv) for mp in range(2)]
    ahead = [scores(*units[v]) for v in range(ATTN_LOOKAHEAD)]
    for u, (d, mp) in enumerate(units):
        s, const = ahead.pop(0)
        if u + ATTN_LOOKAHEAD < len(units):
            ahead.append(scores(*units[u + ATTN_LOOKAHEAD]))
        vt = vt_ref[0, 0, lax.rem(i + d, n_kv)]
        smax = jnp.max(s, axis=0, keepdims=True)
        if const is not None:
            smax = smax + const
        m_new = smax if m[mp] is None else jnp.maximum(m[mp], smax)
        shift = m_new if const is None else m_new - const
        pv = jnp.dot(vt, jnp.exp2(s - shift).astype(BF16), preferred_element_type=F32)
        acc[mp] = pv if m[mp] is None else jnp.exp2(m[mp] - m_new) * acc[mp] + pv
        m[mp] = m_new
    num = [a[:DIFF_V_DIM] / a[DIFF_V_DIM:DIFF_V_DIM + 1] for a in acc]
    lf = lam_ref[...]
    lam = (jnp.exp(jnp.sum(lf[0:1] * lf[1:2], axis=-1, keepdims=True))
           - jnp.exp(jnp.sum(lf[2:3] * lf[3:4], axis=-1, keepdims=True)) + lam_init)
    o = num[0] - lam * num[1]
    y = (o * lax.rsqrt(jnp.mean(o * o, axis=0, keepdims=True) + RMS_EPS) * g_ref[...]
         * (1.0 - lam_init))
    o_ref[...] = y.T.astype(BF16)


def _attention(q, k, vt, bias, table, lam, subln_g, lam_init, batch, seq):
    tt = ATTN_TILE
    nq = seq // tt
    m = batch * seq
    kern = functools.partial(_attn_kernel, lam_init=lam_init, n_kv=nq)
    return pl.pallas_call(
        kern,
        grid=(DIFF_HEADS, batch, nq),
        in_specs=[
            pl.BlockSpec(memory_space=pltpu.SMEM),
            pl.BlockSpec((tt, DIFF_V_DIM), lambda h, b, i: (b * nq + i, h)),
            pl.BlockSpec((seq, DIFF_V_DIM), lambda h, b, i: (b, h)),
            pl.BlockSpec((1, 1, nq, VT_ROWS, tt), lambda h, b, i: (b, h, 0, 0, 0)),
            pl.BlockSpec((1, 5, tt, tt), lambda h, b, i: (h, 0, 0, 0)),
            _const_spec((4, DIFF_QK_DIM)),
            _const_spec((DIFF_V_DIM, 1)),
        ],
        out_specs=pl.BlockSpec((tt, DIFF_V_DIM), lambda h, b, i: (b * nq + i, h)),
        out_shape=jax.ShapeDtypeStruct((m, DIFF_WIDTH), BF16),
        compiler_params=pltpu.CompilerParams(
            dimension_semantics=("arbitrary", "arbitrary", "arbitrary"),
            vmem_limit_bytes=VMEM_LIMIT),
        name="diff_attention",
    )(table, q, k, vt, bias, lam, subln_g)


def _swiglu_residual(x, g_ref, w1_ref, w3_ref, w2_ref):
    hn = _rms(x, g_ref[...]).astype(BF16)
    acc = x
    for c in range(D_FF // FF_CHUNK):
        cols = slice(c * FF_CHUNK, (c + 1) * FF_CHUNK)
        a = jnp.dot(hn, w1_ref[:, cols], preferred_element_type=F32)
        b = jnp.dot(hn, w3_ref[:, cols], preferred_element_type=F32)
        u = (a * jax.nn.sigmoid(a) * b).astype(BF16)
        acc = acc + jnp.dot(u, w2_ref[cols, :], preferred_element_type=F32)
    return acc


def _ffn_kernel(x_ref, g_ref, w1_ref, w3_ref, w2_ref, gf_ref, o_ref, *, final_norm):
    acc = _swiglu_residual(x_ref[...], g_ref, w1_ref, w3_ref, w2_ref)
    if final_norm:
        acc = _rms(acc, gf_ref[...])
    o_ref[...] = acc


def _ffn(x, g, w1, w3, w2, gf, final_norm):
    m = x.shape[0]
    tm = ROW_TILE
    return pl.pallas_call(
        functools.partial(_ffn_kernel, final_norm=final_norm),
        grid=(m // tm,),
        in_specs=[
            pl.BlockSpec((tm, D_MODEL), lambda i: (i, 0)),
            _const_spec((1, D_MODEL)),
            _const_spec((D_MODEL, D_FF)),
            _const_spec((D_MODEL, D_FF)),
            _const_spec((D_FF, D_MODEL)),
            _const_spec((1, D_MODEL)),
        ],
        out_specs=pl.BlockSpec((tm, D_MODEL), lambda i: (i, 0)),
        out_shape=jax.ShapeDtypeStruct((m, D_MODEL), F32),
        compiler_params=pltpu.CompilerParams(
            dimension_semantics=("arbitrary",), vmem_limit_bytes=VMEM_LIMIT),
        name="swiglu_ffn",
    )(x, g, w1, w3, w2, gf)


def _even_tail_kernel(z_ref, x_ref, a_ref, sc_ref, wo_ref, g_ref, w1_ref, w3_ref, w2_ref,
                      o_ref):
    nb = z_ref.shape[0]
    rows = nb * DFT_RADIX * DFT_RADIX
    f = jnp.concatenate(
        [jnp.dot(sc_ref[...], z_ref[b].reshape(sc_ref.shape[1], FOURIER_WIDTH),
                 preferred_element_type=F32).astype(BF16) for b in range(nb)], axis=0)
    x = x_ref[...].reshape(rows, D_MODEL)
    mix = jnp.concatenate([f, a_ref[...].reshape(rows, DIFF_WIDTH)], axis=1)
    h = x + jnp.dot(mix, wo_ref[...], preferred_element_type=F32)
    o_ref[...] = _swiglu_residual(h, g_ref, w1_ref, w3_ref, w2_ref).reshape(o_ref.shape)


def _even_tail(z, x, att, stage_c, w_out, g, w1, w3, w2, batch):
    r = DFT_RADIX
    nb = EVEN_TAIL_BATCHES
    c = FOURIER_WIDTH

    def tiles(width):
        return pl.BlockSpec((nb, r, None, r, width), lambda d, b: (b, 0, d, 0, 0))

    out = pl.pallas_call(
        _even_tail_kernel,
        grid=(r, batch // nb),
        in_specs=[
            pl.BlockSpec((nb, 2, r, None, r, c), lambda d, b: (b, 0, 0, d, 0, 0)),
            tiles(D_MODEL),
            tiles(DIFF_WIDTH),
            _const_spec((r * r, 2 * r * r)),
            _const_spec((FOURIER_WIDTH + DIFF_WIDTH, D_MODEL)),
            _const_spec((1, D_MODEL)),
            _const_spec((D_MODEL, D_FF)),
            _const_spec((D_MODEL, D_FF)),
            _const_spec((D_FF, D_MODEL)),
        ],
        out_specs=tiles(D_MODEL),
        out_shape=jax.ShapeDtypeStruct((batch, r, r, r, D_MODEL), F32),
        compiler_params=pltpu.CompilerParams(
            dimension_semantics=("arbitrary", "arbitrary"), vmem_limit_bytes=VMEM_LIMIT),
        name="even_tail",
    )(z, x.reshape(batch, r, r, r, D_MODEL), att.reshape(batch, r, r, r, DIFF_WIDTH),
      stage_c, w_out, g, w1, w3, w2)
    return out.reshape(batch * SEQ_LEN, D_MODEL)


def _odd_kernel(x_ref, g_ref, wuv_ref, gv_ref, ws_ref, bs_ref, wo_ref, o_ref, y_ref):
    tm = ROW_TILE
    subs = [slice(t * tm, (t + 1) * tm) for t in range(x_ref.shape[0] // tm)]
    zs = [jnp.dot(_rms(x_ref[rows, :], g_ref[...]).astype(BF16), wuv_ref[...],
                  preferred_element_type=F32) for rows in subs]
    gd = SGU_WIDTH // SGU_GROUPS
    for rows, z in zip(subs, zs):
        z = 0.5 * z * (1.0 + lax.erf(z * math.sqrt(0.5)))
        u = z[:, :SGU_WIDTH]
        vn = _rms(z[:, SGU_WIDTH:], gv_ref[...]).astype(BF16)
        for r in range(tm // SGU_CHUNK):
            chunk = slice(r * SGU_CHUNK, (r + 1) * SGU_CHUNK)
            out_rows = slice(rows.start + chunk.start, rows.start + chunk.stop)
            for g in range(SGU_GROUPS):
                cols = slice(g * gd, (g + 1) * gd)
                sv = jnp.dot(ws_ref[g], vn[chunk, cols], preferred_element_type=F32) + bs_ref[g]
                y_ref[out_rows, cols] = (u[chunk, cols] * sv).astype(BF16)
        o_ref[rows, :] = x_ref[rows, :] + jnp.dot(y_ref[rows, :], wo_ref[...],
                                                  preferred_element_type=F32)


def _odd_mixer(x, g, wuv, gv, ws, bs, wo):
    m = x.shape[0]
    tm = SGU_SUBTILES * ROW_TILE
    gd = SGU_WIDTH // SGU_GROUPS
    return pl.pallas_call(
        _odd_kernel,
        grid=(m // tm,),
        in_specs=[
            pl.BlockSpec((tm, D_MODEL), lambda i: (i, 0)),
            _const_spec((1, D_MODEL)),
            _const_spec((D_MODEL, 2 * SGU_WIDTH)),
            _const_spec((1, SGU_WIDTH)),
            _const_spec((SGU_GROUPS, SGU_CHUNK, SGU_CHUNK)),
            _const_spec((SGU_GROUPS, SGU_CHUNK, gd)),
            _const_spec((SGU_WIDTH, D_MODEL)),
        ],
        out_specs=pl.BlockSpec((tm, D_MODEL), lambda i: (i, 0)),
        out_shape=jax.ShapeDtypeStruct((m, D_MODEL), F32),
        scratch_shapes=[pltpu.VMEM((tm, SGU_WIDTH), BF16)],
        compiler_params=pltpu.CompilerParams(
            dimension_semantics=("arbitrary",), vmem_limit_bytes=VMEM_LIMIT),
        name="sgu_mixer",
    )(x, g, wuv, gv, ws, bs, wo)


def _cast_kernel(*refs):
    n = len(refs) // 2
    for src, dst in zip(refs[:n], refs[n:]):
        dst[...] = src[...].astype(BF16)


def _cast_weights(items):
    steps = CAST_STEPS
    in_specs, out_specs, out_shape = [], [], []
    for arr, layer in items:
        _, rows, cols = arr.shape
        blk = rows // steps
        assert blk * steps == rows and blk % 16 == 0
        in_specs.append(pl.BlockSpec((None, blk, cols), lambda s, layer=layer: (layer, s, 0)))
        out_specs.append(pl.BlockSpec((blk, cols), lambda s: (s, 0)))
        out_shape.append(jax.ShapeDtypeStruct((rows, cols), BF16))
    return pl.pallas_call(
        _cast_kernel, grid=(steps,), in_specs=in_specs, out_specs=out_specs,
        out_shape=out_shape,
        compiler_params=pltpu.CompilerParams(
            dimension_semantics=("arbitrary",), vmem_limit_bytes=VMEM_LIMIT),
        name="cast_weights",
    )(*[arr for arr, _ in items])


def kernel(x, rel_bias_table, norm_mix_g, norm_ffn_g, even_w_in, even_w_out, diff_lambda,
           diff_subln_g, odd_w_uv, odd_v_norm_g, odd_w_s, odd_b_s, odd_w_out, ffn_w1, ffn_w3,
           ffn_w2, final_norm_g):
    batch, seq, d = x.shape
    assert (seq, d) == (SEQ_LEN, D_MODEL)
    depth = norm_mix_g.shape[0]
    assert depth % 2 == 0
    m = batch * seq
    h = x.reshape(m, d)
    stage_a, stage_b, stage_c, cdft = _dft_constants()

    items, slot = [], {}
    def want(name, arr, layer):
        slot[name, layer] = len(items)
        items.append((arr.reshape(arr.shape[0], -1, arr.shape[-1]), layer))
    for i in range(depth):
        jx = i // 2
        if i % 2 == 0:
            want('w_in', even_w_in, jx)
            want('w_out_even', even_w_out, jx)
        else:
            want('w_uv', odd_w_uv, jx)
            want('w_s', odd_w_s, jx)
            want('w_out_odd', odd_w_out, jx)
        want('w1', ffn_w1, i)
        want('w3', ffn_w3, i)
        want('w2', ffn_w2, i)
    cast = _cast_weights(items)
    w = lambda name, layer: cast[slot[name, layer]]

    for i in range(depth):
        jx = i // 2
        g_mix = norm_mix_g[i].reshape(1, d)
        g_ffn = norm_ffn_g[i].reshape(1, d)
        if i % 2 == 0:
            lam_init = 0.8 - 0.6 * math.exp(-0.3 * i)
            wv_t = even_w_in[jx][:, FOURIER_WIDTH + 2 * DIFF_WIDTH:].T.astype(BF16)
            xcs, q, k, vt = _even_in(h, g_mix, w('w_in', jx), wv_t, cdft, batch, seq)
            z = _fourier_ab(xcs, stage_a, stage_b, batch)
            bias = _bias_tiles(rel_bias_table)
            att = _attention(q, k, vt, bias, rel_bias_table, diff_lambda[jx],
                             diff_subln_g[jx].reshape(DIFF_V_DIM, 1), lam_init, batch, seq)
            h = _even_tail(z, h, att, stage_c, w('w_out_even', jx), g_ffn,
                           w('w1', i), w('w3', i), w('w2', i), batch)
        else:
            bs = jnp.broadcast_to(odd_b_s[jx][:, :, None],
                                  (SGU_GROUPS, SGU_CHUNK, SGU_WIDTH // SGU_GROUPS))
            h = _odd_mixer(h, g_mix, w('w_uv', jx), odd_v_norm_g[jx].reshape(1, SGU_WIDTH),
                           w('w_s', jx).reshape(SGU_GROUPS, SGU_CHUNK, SGU_CHUNK), bs,
                           w('w_out_odd', jx))
            h = _ffn(h, g_ffn, w('w1', i), w('w3', i), w('w2', i),
                     final_norm_g.reshape(1, d), i == depth - 1)
    return h.reshape(batch, seq, d)
```

```python
import functools
import math

import numpy as np
import jax
import jax.numpy as jnp
from jax import lax
from jax.experimental import pallas as pl
from jax.experimental.pallas import tpu as pltpu

F32 = jnp.float32
BF16 = jnp.bfloat16

D_MODEL = 1024
FOURIER_GROUP_DIM = 128
FOURIER_WIDTH = 512
DIFF_HEADS = 4
DIFF_QK_DIM = 64
DIFF_V_DIM = 128
DIFF_WIDTH = 512
REL_BUCKETS = 32
REL_MAX_DIST = 128
SGU_CHUNK = 128
SGU_GROUPS = 8
SGU_WIDTH = 1024
D_FF = 2816
RMS_EPS = 1e-6
LOG2E = math.log2(math.e)

DFT_RADIX = 16
SEQ_LEN = DFT_RADIX ** 3

ROW_TILE = 512
DFT_SLAB_LANES = 256
SGU_SUBTILES = 2
EVEN_TAIL_BATCHES = 2
ATTN_TILE = 512
ATTN_LOOKAHEAD = 2
VT_ROWS = DIFF_V_DIM + 16
FF_CHUNK = 256
CAST_STEPS = 8
VMEM_LIMIT = 56 * 1024 * 1024


def _rms(x, g):
    return x * lax.rsqrt(jnp.mean(x * x, axis=-1, keepdims=True) + RMS_EPS) * g


def _const_spec(shape):
    return pl.BlockSpec(shape, lambda *_: (0,) * len(shape))


def _dft_constants():
    r = DFT_RADIX
    n = SEQ_LEN
    i = np.arange(r)
    f = np.exp(-2j * np.pi * np.outer(i, i) / r)
    eye = np.eye(r)

    def real_rep(g):
        return np.block([[g.real, -g.imag], [g.imag, g.real]])

    stage_a, stage_b = [], []
    for t in range(r):
        tw = np.exp(-2j * np.pi * t * i / (r * r))
        g = np.einsum('pr,kn,k->pknr', eye, f, tw).reshape(r * r, r * r)
        stage_a.append(real_rep(g))
    for rr in range(r):
        tw = np.exp(-2j * np.pi * rr * (i[None, :] + r * i[:, None]) / n)
        g = np.einsum('qk,jt,jq->jqtk', eye, f, tw).reshape(r * r, r * r)
        stage_b.append(real_rep(g))
    scale = 1.0 / math.sqrt(n * FOURIER_GROUP_DIM)
    g = np.einsum('qk,jr->jqrk', eye, f).reshape(r * r, r * r) * scale
    stage_c = np.concatenate([g.real, -g.imag], axis=1)
    cidx = np.arange(FOURIER_GROUP_DIM)
    cang = 2.0 * np.pi * np.outer(cidx, cidx) / FOURIER_GROUP_DIM
    eye2 = np.eye(2)
    cdft = np.stack([np.kron(eye2, np.cos(cang)), -np.kron(eye2, np.sin(cang))])
    return (jnp.asarray(np.stack(stage_a), BF16), jnp.asarray(np.stack(stage_b), BF16),
            jnp.asarray(stage_c, BF16), jnp.asarray(cdft, BF16))


def _even_in_kernel(x_ref, g_ref, w_ref, wvt_ref, cdft_ref, xcs_ref, q_ref, k_ref, vt_ref):
    hn = _rms(x_ref[...], g_ref[...]).astype(BF16)
    n_fqk = FOURIER_WIDTH + 2 * DIFF_WIDTH
    z = jnp.dot(hn, w_ref[:, :n_fqk], preferred_element_type=F32)
    vt = lax.dot_general(wvt_ref[...], hn, (((1,), (1,)), ((), ())),
                         preferred_element_type=F32)
    pad_rows = lax.broadcasted_iota(jnp.int32, (VT_ROWS - DIFF_V_DIM, vt.shape[1]), 0)
    ones_tile = jnp.where(pad_rows == 0, 1.0, 0.0).astype(BF16)
    for hh in range(DIFF_HEADS):
        vt_ref[0, hh, 0, :DIFF_V_DIM] = vt[hh * DIFF_V_DIM:(hh + 1) * DIFF_V_DIM].astype(BF16)
        vt_ref[0, hh, 0, DIFF_V_DIM:] = ones_tile
    zf = z[:, :FOURIER_WIDTH].astype(BF16)
    for p in range(FOURIER_WIDTH // 256):
        zp = zf[:, p * 256:(p + 1) * 256]
        xcs_ref[0, 0, :, p * 256:(p + 1) * 256] = jnp.dot(
            zp, cdft_ref[0], preferred_element_type=F32).astype(BF16)
        xcs_ref[0, 1, :, p * 256:(p + 1) * 256] = jnp.dot(
            zp, cdft_ref[1], preferred_element_type=F32).astype(BF16)
    o = FOURIER_WIDTH
    q_ref[...] = (z[:, o:o + 512] * (DIFF_QK_DIM ** -0.5 * LOG2E)).astype(BF16)
    k_ref[...] = z[:, o + 512:o + 1024].astype(BF16)


def _even_in(x, g, w, wvt, cdft, batch, seq):
    m = x.shape[0]
    tm = ROW_TILE
    assert tm == ATTN_TILE
    per_b = seq // tm
    n_out = w.shape[1]
    return pl.pallas_call(
        _even_in_kernel,
        grid=(m // tm,),
        in_specs=[
            pl.BlockSpec((tm, D_MODEL), lambda i: (i, 0)),
            _const_spec((1, D_MODEL)),
            _const_spec((D_MODEL, n_out)),
            _const_spec((DIFF_WIDTH, D_MODEL)),
            _const_spec((2, 256, 256)),
        ],
        out_specs=[
            pl.BlockSpec((1, 2, tm, FOURIER_WIDTH), lambda i: (i // per_b, 0, i % per_b, 0)),
            pl.BlockSpec((tm, 512), lambda i: (i, 0)),
            pl.BlockSpec((tm, 512), lambda i: (i, 0)),
            pl.BlockSpec((1, DIFF_HEADS, 1, VT_ROWS, tm),
                         lambda i: (i // per_b, 0, i % per_b, 0, 0)),
        ],
        out_shape=[
            jax.ShapeDtypeStruct((batch, 2, seq, FOURIER_WIDTH), BF16),
            jax.ShapeDtypeStruct((m, 512), BF16),
            jax.ShapeDtypeStruct((m, 512), BF16),
            jax.ShapeDtypeStruct((batch, DIFF_HEADS, per_b, VT_ROWS, tm), BF16),
        ],
        compiler_params=pltpu.CompilerParams(
            dimension_semantics=("arbitrary",), vmem_limit_bytes=VMEM_LIMIT),
        name="even_in_proj",
    )(x, g, w, wvt, cdft)


def _dft_ab_kernel(x_ref, wa_ref, wb_ref, o_ref, y_ref):
    r = DFT_RADIX
    lanes = x_ref.shape[-1]
    for t in range(r):
        blk = x_ref[0, :, :, t].reshape(2 * r * r, lanes)
        y = jnp.dot(wa_ref[t], blk, preferred_element_type=F32)
        y_ref[:, :, t] = y.astype(BF16).reshape(2, r, r, lanes)
    for rr in range(r):
        blk = y_ref[:, rr].reshape(2 * r * r, lanes)
        z = jnp.dot(wb_ref[rr], blk, preferred_element_type=F32)
        o_ref[0, :, rr] = z.astype(BF16).reshape(2, r, r, lanes)


def _fourier_ab(xcs, stage_a, stage_b, batch):
    r = DFT_RADIX
    c = FOURIER_WIDTH
    nn = 2 * r * r
    lanes = DFT_SLAB_LANES
    slab = pl.BlockSpec((1, 2, r, r, r, lanes), lambda b, j: (b, 0, 0, 0, 0, j))
    return pl.pallas_call(
        _dft_ab_kernel,
        grid=(batch, c // lanes),
        in_specs=[slab, _const_spec((r, nn, nn)), _const_spec((r, nn, nn))],
        out_specs=slab,
        out_shape=jax.ShapeDtypeStruct((batch, 2, r, r, r, c), BF16),
        scratch_shapes=[pltpu.VMEM((2, r, r, r, lanes), BF16)],
        compiler_params=pltpu.CompilerParams(
            dimension_semantics=("arbitrary", "arbitrary"), vmem_limit_bytes=VMEM_LIMIT),
        name="seq_dft_stages_ab",
    )(xcs.reshape(batch, 2, r, r, r, c), stage_a, stage_b)


def _bias_tiles_kernel(tab_ref, o_ref):
    h = pl.program_id(0)
    tt = ATTN_TILE
    half = REL_BUCKETS // 2
    max_exact = half // 2
    o_ref[0, 0] = jnp.full((tt, tt), tab_ref[half - 1, h] * LOG2E, F32)
    o_ref[0, 4] = jnp.full((tt, tt), tab_ref[REL_BUCKETS - 1, h] * LOG2E, F32)
    wide = 2 * tt
    lane = lax.broadcasted_iota(jnp.int32, (8, wide), 1)
    diff = jnp.where(lane < tt, -lane, wide - lane)
    for t in range(1, 4):
        rel = diff + (t - 2) * tt
        ret = jnp.where(rel > 0, half, 0)
        n = jnp.abs(rel)
        nf = jnp.maximum(n, 1).astype(F32)
        large = max_exact + (jnp.log(nf / max_exact) / math.log(REL_MAX_DIST / max_exact)
                             * (half - max_exact)).astype(jnp.int32)
        large = jnp.minimum(large, half - 1)
        bucket = ret + jnp.where(n < max_exact, n, large)
        lo = half if t > 2 else 0
        hi = half if t < 2 else REL_BUCKETS
        acc = jnp.zeros((8, wide), F32)
        for bk in range(lo, hi):
            acc = jnp.where(bucket == bk, tab_ref[bk, h], acc)
        rows = jnp.broadcast_to(acc[0:1] * LOG2E, (tt, wide))
        o_ref[0, t] = pltpu.roll(rows, 0, 1, stride=1, stride_axis=0)[:, :tt]


def _bias_tiles(table):
    tt = ATTN_TILE
    return pl.pallas_call(
        _bias_tiles_kernel,
        grid=(DIFF_HEADS,),
        in_specs=[pl.BlockSpec(memory_space=pltpu.SMEM)],
        out_specs=pl.BlockSpec((1, 5, tt, tt), lambda h: (h, 0, 0, 0)),
        out_shape=jax.ShapeDtypeStruct((DIFF_HEADS, 5, tt, tt), F32),
        compiler_params=pltpu.CompilerParams(
            dimension_semantics=("arbitrary",), vmem_limit_bytes=VMEM_LIMIT),
        name="rel_bias_tiles",
    )(table)


def _attn_kernel(tab_ref, q_ref, k_ref, vt_ref, bias_ref, lam_ref, g_ref, o_ref, *,
                 lam_init, n_kv):
    h = pl.program_id(0)
    i = pl.program_id(2)
    tt = ATTN_TILE
    qt = q_ref[...].astype(F32).T
    row = lax.broadcasted_iota(jnp.int32, qt.shape, 0)
    qm = [jnp.where(row < DIFF_QK_DIM, qt, 0.0).astype(BF16),
          jnp.where(row >= DIFF_QK_DIM, qt, 0.0).astype(BF16)]
    c_after = tab_ref[REL_BUCKETS - 1, h] * LOG2E
    c_before = tab_ref[REL_BUCKETS // 2 - 1, h] * LOG2E

    def scores(d, mp):
        j = lax.rem(i + d, n_kv)
        kt = k_ref[pl.ds(pl.multiple_of(j * tt, tt), tt), :]
        s = jnp.dot(kt, qm[mp], preferred_element_type=F32)
        if 2 <= d <= n_kv - 2:
            return s, jnp.where(i + d < n_kv, c_after, c_before)
        return s + bias_ref[0, jnp.clip(j - i, -2, 2) + 2], None

    m = [None, None]
    acc = [None, None]
    units = [(d, mp) for d in range(n_kv) for mp in range(2)]
    ahead = [scores(*units[v]) for v in range(ATTN_LOOKAHEAD)]
    for u, (d, mp) in enumerate(units):
        s, const = ahead.pop(0)
        if u + ATTN_LOOKAHEAD < len(units):
            ahead.append(scores(*units[u + ATTN_LOOKAHEAD]))
        vt = vt_ref[0, 0, lax.rem(i + d, n_kv)]
        smax = jnp.max(s, axis=0, keepdims=True)
        if const is not None:
            smax = smax + const
        m_new = smax if m[mp] is None else jnp.maximum(m[mp], smax)
        shift = m_new if const is None else m_new - const
        pv = jnp.dot(vt, jnp.exp2(s - shift).astype(BF16), preferred_element_type=F32)
        acc[mp] = pv if m[mp] is None else jnp.exp2(m[mp] - m_new) * acc[mp] + pv
        m[mp] = m_new
    num = [a[:DIFF_V_DIM] / a[DIFF_V_DIM:DIFF_V_DIM + 1] for a in acc]
    lf = lam_ref[...]
    lam = (jnp.exp(jnp.sum(lf[0:1] * lf[1:2], axis=-1, keepdims=True))
           - jnp.exp(jnp.sum(lf[2:3] * lf[3:4], axis=-1, keepdims=True)) + lam_init)
    o = num[0] - lam * num[1]
    y = (o * lax.rsqrt(jnp.mean(o * o, axis=0, keepdims=True) + RMS_EPS) * g_ref[...]
         * (1.0 - lam_init))
    o_ref[...] = y.T.astype(BF16)


def _attention(q, k, vt, bias, table, lam, subln_g, lam_init, batch, seq):
    tt = ATTN_TILE
    nq = seq // tt
    m = batch * seq
    kern = functools.partial(_attn_kernel, lam_init=lam_init, n_kv=nq)
    return pl.pallas_call(
        kern,
        grid=(DIFF_HEADS, batch, nq),
        in_specs=[
            pl.BlockSpec(memory_space=pltpu.SMEM),
            pl.BlockSpec((tt, DIFF_V_DIM), lambda h, b, i: (b * nq + i, h)),
            pl.BlockSpec((seq, DIFF_V_DIM), lambda h, b, i: (b, h)),
            pl.BlockSpec((1, 1, nq, VT_ROWS, tt), lambda h, b, i: (b, h, 0, 0, 0)),
            pl.BlockSpec((1, 5, tt, tt), lambda h, b, i: (h, 0, 0, 0)),
            _const_spec((4, DIFF_QK_DIM)),
            _const_spec((DIFF_V_DIM, 1)),
        ],
        out_specs=pl.BlockSpec((tt, DIFF_V_DIM), lambda h, b, i: (b * nq + i, h)),
        out_shape=jax.ShapeDtypeStruct((m, DIFF_WIDTH), BF16),
        compiler_params=pltpu.CompilerParams(
            dimension_semantics=("arbitrary", "arbitrary", "arbitrary"),
            vmem_limit_bytes=VMEM_LIMIT),
        name="diff_attention",
    )(table, q, k, vt, bias, lam, subln_g)


def _swiglu_residual(x, g_ref, w1_ref, w3_ref, w2_ref):
    hn = _rms(x, g_ref[...]).astype(BF16)
    acc = x
    for c in range(D_FF // FF_CHUNK):
        cols = slice(c * FF_CHUNK, (c + 1) * FF_CHUNK)
        a = jnp.dot(hn, w1_ref[:, cols], preferred_element_type=F32)
        b = jnp.dot(hn, w3_ref[:, cols], preferred_element_type=F32)
        u = (a * jax.nn.sigmoid(a) * b).astype(BF16)
        acc = acc + jnp.dot(u, w2_ref[cols, :], preferred_element_type=F32)
    return acc


def _ffn_kernel(x_ref, g_ref, w1_ref, w3_ref, w2_ref, gf_ref, o_ref, *, final_norm):
    acc = _swiglu_residual(x_ref[...], g_ref, w1_ref, w3_ref, w2_ref)
    if final_norm:
        acc = _rms(acc, gf_ref[...])
    o_ref[...] = acc


def _ffn(x, g, w1, w3, w2, gf, final_norm):
    m = x.shape[0]
    tm = ROW_TILE
    return pl.pallas_call(
        functools.partial(_ffn_kernel, final_norm=final_norm),
        grid=(m // tm,),
        in_specs=[
            pl.BlockSpec((tm, D_MODEL), lambda i: (i, 0)),
            _const_spec((1, D_MODEL)),
            _const_spec((D_MODEL, D_FF)),
            _const_spec((D_MODEL, D_FF)),
            _const_spec((D_FF, D_MODEL)),
            _const_spec((1, D_MODEL)),
        ],
        out_specs=pl.BlockSpec((tm, D_MODEL), lambda i: (i, 0)),
        out_shape=jax.ShapeDtypeStruct((m, D_MODEL), F32),
        compiler_params=pltpu.CompilerParams(
            dimension_semantics=("arbitrary",), vmem_limit_bytes=VMEM_LIMIT),
        name="swiglu_ffn",
    )(x, g, w1, w3, w2, gf)


def _even_tail_kernel(z_ref, x_ref, a_ref, sc_ref, wo_ref, g_ref, w1_ref, w3_ref, w2_ref,
                      o_ref):
    nb = z_ref.shape[0]
    rows = nb * DFT_RADIX * DFT_RADIX
    f = jnp.concatenate(
        [jnp.dot(sc_ref[...], z_ref[b].reshape(sc_ref.shape[1], FOURIER_WIDTH),
                 preferred_element_type=F32).astype(BF16) for b in range(nb)], axis=0)
    x = x_ref[...].reshape(rows, D_MODEL)
    mix = jnp.concatenate([f, a_ref[...].reshape(rows, DIFF_WIDTH)], axis=1)
    h = x + jnp.dot(mix, wo_ref[...], preferred_element_type=F32)
    o_ref[...] = _swiglu_residual(h, g_ref, w1_ref, w3_ref, w2_ref).reshape(o_ref.shape)


def _even_tail(z, x, att, stage_c, w_out, g, w1, w3, w2, batch):
    r = DFT_RADIX
    nb = EVEN_TAIL_BATCHES
    c = FOURIER_WIDTH

    def tiles(width):
        return pl.BlockSpec((nb, r, None, r, width), lambda d, b: (b, 0, d, 0, 0))

    out = pl.pallas_call(
        _even_tail_kernel,
        grid=(r, batch // nb),
        in_specs=[
            pl.BlockSpec((nb, 2, r, None, r, c), lambda d, b: (b, 0, 0, d, 0, 0)),
            tiles(D_MODEL),
            tiles(DIFF_WIDTH),
            _const_spec((r * r, 2 * r * r)),
            _const_spec((FOURIER_WIDTH + DIFF_WIDTH, D_MODEL)),
            _const_spec((1, D_MODEL)),
            _const_spec((D_MODEL, D_FF)),
            _const_spec((D_MODEL, D_FF)),
            _const_spec((D_FF, D_MODEL)),
        ],
        out_specs=tiles(D_MODEL),
        out_shape=jax.ShapeDtypeStruct((batch, r, r, r, D_MODEL), F32),
        compiler_params=pltpu.CompilerParams(
            dimension_semantics=("arbitrary", "arbitrary"), vmem_limit_bytes=VMEM_LIMIT),
        name="even_tail",
    )(z, x.reshape(batch, r, r, r, D_MODEL), att.reshape(batch, r, r, r, DIFF_WIDTH),
      stage_c, w_out, g, w1, w3, w2)
    return out.reshape(batch * SEQ_LEN, D_MODEL)


def _odd_kernel(x_ref, g_ref, wuv_ref, gv_ref, ws_ref, bs_ref, wo_ref, o_ref, y_ref):
    tm = ROW_TILE
    subs = [slice(t * tm, (t + 1) * tm) for t in range(x_ref.shape[0] // tm)]
    zs = [jnp.dot(_rms(x_ref[rows, :], g_ref[...]).astype(BF16), wuv_ref[...],
                  preferred_element_type=F32) for rows in subs]
    gd = SGU_WIDTH // SGU_GROUPS
    for rows, z in zip(subs, zs):
        z = 0.5 * z * (1.0 + lax.erf(z * math.sqrt(0.5)))
        u = z[:, :SGU_WIDTH]
        vn = _rms(z[:, SGU_WIDTH:], gv_ref[...]).astype(BF16)
        for r in range(tm // SGU_CHUNK):
            chunk = slice(r * SGU_CHUNK, (r + 1) * SGU_CHUNK)
            out_rows = slice(rows.start + chunk.start, rows.start + chunk.stop)
            for g in range(SGU_GROUPS):
                cols = slice(g * gd, (g + 1) * gd)
                sv = jnp.dot(ws_ref[g], vn[chunk, cols], preferred_element_type=F32) + bs_ref[g]
                y_ref[out_rows, cols] = (u[chunk, cols] * sv).astype(BF16)
        o_ref[rows, :] = x_ref[rows, :] + jnp.dot(y_ref[rows, :], wo_ref[...],
                                                  preferred_element_type=F32)


def _odd_mixer(x, g, wuv, gv, ws, bs, wo):
    m = x.shape[0]
    tm = SGU_SUBTILES * ROW_TILE
    gd = SGU_WIDTH // SGU_GROUPS
    return pl.pallas_call(
        _odd_kernel,
        grid=(m // tm,),
        in_specs=[
            pl.BlockSpec((tm, D_MODEL), lambda i: (i, 0)),
            _const_spec((1, D_MODEL)),
            _const_spec((D_MODEL, 2 * SGU_WIDTH)),
            _const_spec((1, SGU_WIDTH)),
            _const_spec((SGU_GROUPS, SGU_CHUNK, SGU_CHUNK)),
            _const_spec((SGU_GROUPS, SGU_CHUNK, gd)),
            _const_spec((SGU_WIDTH, D_MODEL)),
        ],
        out_specs=pl.BlockSpec((tm, D_MODEL), lambda i: (i, 0)),
        out_shape=jax.ShapeDtypeStruct((m, D_MODEL), F32),
        scratch_shapes=[pltpu.VMEM((tm, SGU_WIDTH), BF16)],
        compiler_params=pltpu.CompilerParams(
            dimension_semantics=("arbitrary",), vmem_limit_bytes=VMEM_LIMIT),
        name="sgu_mixer",
    )(x, g, wuv, gv, ws, bs, wo)


def _cast_kernel(*refs):
    n = len(refs) // 2
    for src, dst in zip(refs[:n], refs[n:]):
        dst[...] = src[...].astype(BF16)


def _cast_weights(items):
    steps = CAST_STEPS
    in_specs, out_specs, out_shape = [], [], []
    for arr, layer in items:
        _, rows, cols = arr.shape
        blk = rows // steps
        assert blk * steps == rows and blk % 16 == 0
        in_specs.append(pl.BlockSpec((None, blk, cols), lambda s, layer=layer: (layer, s, 0)))
        out_specs.append(pl.BlockSpec((blk, cols), lambda s: (s, 0)))
        out_shape.append(jax.ShapeDtypeStruct((rows, cols), BF16))
    return pl.pallas_call(
        _cast_kernel, grid=(steps,), in_specs=in_specs, out_specs=out_specs,
        out_shape=out_shape,
        compiler_params=pltpu.CompilerParams(
            dimension_semantics=("arbitrary",), vmem_limit_bytes=VMEM_LIMIT),
        name="cast_weights",
    )(*[arr for arr, _ in items])


def kernel(x, rel_bias_table, norm_mix_g, norm_ffn_g, even_w_in, even_w_out, diff_lambda,
           diff_subln_g, odd_w_uv, odd_v_norm_g, odd_w_s, odd_b_s, odd_w_out, ffn_w1, ffn_w3,
           ffn_w2, final_norm_g):
    batch, seq, d = x.shape
    assert (seq, d) == (SEQ_LEN, D_MODEL)
    depth = norm_mix_g.shape[0]
    assert depth % 2 == 0
    m = batch * seq
    h = x.reshape(m, d)
    stage_a, stage_b, stage_c, cdft = _dft_constants()

    items, slot = [], {}
    def want(name, arr, layer):
        slot[name, layer] = len(items)
        items.append((arr.reshape(arr.shape[0], -1, arr.shape[-1]), layer))
    for i in range(depth):
        jx = i // 2
        if i % 2 == 0:
            want('w_in', even_w_in, jx)
            want('w_out_even', even_w_out, jx)
        else:
            want('w_uv', odd_w_uv, jx)
            want('w_s', odd_w_s, jx)
            want('w_out_odd', odd_w_out, jx)
        want('w1', ffn_w1, i)
        want('w3', ffn_w3, i)
        want('w2', ffn_w2, i)
    cast = _cast_weights(items)
    w = lambda name, layer: cast[slot[name, layer]]

    for i in range(depth):
        jx = i // 2
        g_mix = norm_mix_g[i].reshape(1, d)
        g_ffn = norm_ffn_g[i].reshape(1, d)
        if i % 2 == 0:
            lam_init = 0.8 - 0.6 * math.exp(-0.3 * i)
            wv_t = even_w_in[jx][:, FOURIER_WIDTH + 2 * DIFF_WIDTH:].T.astype(BF16)
            xcs, q, k, vt = _even_in(h, g_mix, w('w_in', jx), wv_t, cdft, batch, seq)
            z = _fourier_ab(xcs, stage_a, stage_b, batch)
            bias = _bias_tiles(rel_bias_table)
            att = _attention(q, k, vt, bias, rel_bias_table, diff_lambda[jx],
                             diff_subln_g[jx].reshape(DIFF_V_DIM, 1), lam_init, batch, seq)
            h = _even_tail(z, h, att, stage_c, w('w_out_even', jx), g_ffn,
                           w('w1', i), w('w3', i), w('w2', i), batch)
        else:
            bs = jnp.broadcast_to(odd_b_s[jx][:, :, None],
                                  (SGU_GROUPS, SGU_CHUNK, SGU_WIDTH // SGU_GROUPS))
            h = _odd_mixer(h, g_mix, w('w_uv', jx), odd_v_norm_g[jx].reshape(1, SGU_WIDTH),
                           w('w_s', jx).reshape(SGU_GROUPS, SGU_CHUNK, SGU_CHUNK), bs,
                           w('w_out_odd', jx))
            h = _ffn(h, g_ffn, w('w1', i), w('w3', i), w('w2', i),
                     final_norm_g.reshape(1, d), i == depth - 1)
    return h.reshape(batch, seq, d)
```

```python
import functools
import math

import numpy as np
import jax
import jax.numpy as jnp
from jax import lax
from jax.experimental import pallas as pl
from jax.experimental.pallas import tpu as pltpu

F32 = jnp.float32
BF16 = jnp.bfloat16

D_MODEL = 1024
FOURIER_GROUP_DIM = 128
FOURIER_WIDTH = 512
DIFF_HEADS = 4
DIFF_QK_DIM = 64
DIFF_V_DIM = 128
DIFF_WIDTH = 512
REL_BUCKETS = 32
REL_MAX_DIST = 128
SGU_CHUNK = 128
SGU_GROUPS = 8
SGU_WIDTH = 1024
D_FF = 2816
RMS_EPS = 1e-6
LOG2E = math.log2(math.e)

DFT_RADIX = 16
SEQ_LEN = DFT_RADIX ** 3

ROW_TILE = 512
DFT_SLAB_LANES = 256
SGU_SUBTILES = 2
EVEN_TAIL_BATCHES = 2
ATTN_TILE = 512
ATTN_QTILES = 2
ATTN_LOOKAHEAD = 2
VT_ROWS = DIFF_V_DIM + 16
FF_CHUNK = 256
CAST_STEPS = 8
VMEM_LIMIT = 56 * 1024 * 1024


def _rms(x, g):
    return x * lax.rsqrt(jnp.mean(x * x, axis=-1, keepdims=True) + RMS_EPS) * g


def _const_spec(shape):
    return pl.BlockSpec(shape, lambda *_: (0,) * len(shape))


def _dft_constants():
    r = DFT_RADIX
    n = SEQ_LEN
    i = np.arange(r)
    f = np.exp(-2j * np.pi * np.outer(i, i) / r)
    eye = np.eye(r)

    def real_rep(g):
        return np.block([[g.real, -g.imag], [g.imag, g.real]])

    stage_a, stage_b = [], []
    for t in range(r):
        tw = np.exp(-2j * np.pi * t * i / (r * r))
        g = np.einsum('pr,kn,k->pknr', eye, f, tw).reshape(r * r, r * r)
        stage_a.append(real_rep(g))
    for rr in range(r):
        tw = np.exp(-2j * np.pi * rr * (i[None, :] + r * i[:, None]) / n)
        g = np.einsum('qk,jt,jq->jqtk', eye, f, tw).reshape(r * r, r * r)
        stage_b.append(real_rep(g))
    scale = 1.0 / math.sqrt(n * FOURIER_GROUP_DIM)
    g = np.einsum('qk,jr->jqrk', eye, f).reshape(r * r, r * r) * scale
    stage_c = np.concatenate([g.real, -g.imag], axis=1)
    cidx = np.arange(FOURIER_GROUP_DIM)
    cang = 2.0 * np.pi * np.outer(cidx, cidx) / FOURIER_GROUP_DIM
    eye2 = np.eye(2)
    cdft = np.stack([np.kron(eye2, np.cos(cang)), -np.kron(eye2, np.sin(cang))])
    return (jnp.asarray(np.stack(stage_a), BF16), jnp.asarray(np.stack(stage_b), BF16),
            jnp.asarray(stage_c, BF16), jnp.asarray(cdft, BF16))


def _even_in_kernel(x_ref, g_ref, w_ref, wvt_ref, cdft_ref, xcs_ref, q_ref, k_ref, vt_ref):
    hn = _rms(x_ref[...], g_ref[...]).astype(BF16)
    n_fqk = FOURIER_WIDTH + 2 * DIFF_WIDTH
    z = jnp.dot(hn, w_ref[:, :n_fqk], preferred_element_type=F32)
    vt = lax.dot_general(wvt_ref[...], hn, (((1,), (1,)), ((), ())),
                         preferred_element_type=F32)
    pad_rows = lax.broadcasted_iota(jnp.int32, (VT_ROWS - DIFF_V_DIM, vt.shape[1]), 0)
    ones_tile = jnp.where(pad_rows == 0, 1.0, 0.0).astype(BF16)
    for hh in range(DIFF_HEADS):
        vt_ref[0, hh, 0, :DIFF_V_DIM] = vt[hh * DIFF_V_DIM:(hh + 1) * DIFF_V_DIM].astype(BF16)
        vt_ref[0, hh, 0, DIFF_V_DIM:] = ones_tile
    zf = z[:, :FOURIER_WIDTH].astype(BF16)
    for p in range(FOURIER_WIDTH // 256):
        zp = zf[:, p * 256:(p + 1) * 256]
        xcs_ref[0, 0, :, p * 256:(p + 1) * 256] = jnp.dot(
            zp, cdft_ref[0], preferred_element_type=F32).astype(BF16)
        xcs_ref[0, 1, :, p * 256:(p + 1) * 256] = jnp.dot(
            zp, cdft_ref[1], preferred_element_type=F32).astype(BF16)
    o = FOURIER_WIDTH
    q_ref[...] = (z[:, o:o + 512] * (DIFF_QK_DIM ** -0.5 * LOG2E)).astype(BF16)
    k_ref[...] = z[:, o + 512:o + 1024].astype(BF16)


def _even_in(x, g, w, wvt, cdft, batch, seq):
    m = x.shape[0]
    tm = ROW_TILE
    assert tm == ATTN_TILE
    per_b = seq // tm
    n_out = w.shape[1]
    return pl.pallas_call(
        _even_in_kernel,
        grid=(m // tm,),
        in_specs=[
            pl.BlockSpec((tm, D_MODEL), lambda i: (i, 0)),
            _const_spec((1, D_MODEL)),
            _const_spec((D_MODEL, n_out)),
            _const_spec((DIFF_WIDTH, D_MODEL)),
            _const_spec((2, 256, 256)),
        ],
        out_specs=[
            pl.BlockSpec((1, 2, tm, FOURIER_WIDTH), lambda i: (i // per_b, 0, i % per_b, 0)),
            pl.BlockSpec((tm, 512), lambda i: (i, 0)),
            pl.BlockSpec((tm, 512), lambda i: (i, 0)),
            pl.BlockSpec((1, DIFF_HEADS, 1, VT_ROWS, tm),
                         lambda i: (i // per_b, 0, i % per_b, 0, 0)),
        ],
        out_shape=[
            jax.ShapeDtypeStruct((batch, 2, seq, FOURIER_WIDTH), BF16),
            jax.ShapeDtypeStruct((m, 512), BF16),
            jax.ShapeDtypeStruct((m, 512), BF16),
            jax.ShapeDtypeStruct((batch, DIFF_HEADS, per_b, VT_ROWS, tm), BF16),
        ],
        compiler_params=pltpu.CompilerParams(
            dimension_semantics=("arbitrary",), vmem_limit_bytes=VMEM_LIMIT),
        name="even_in_proj",
    )(x, g, w, wvt, cdft)


def _dft_ab_kernel(x_ref, wa_ref, wb_ref, o_ref, y_ref):
    r = DFT_RADIX
    lanes = x_ref.shape[-1]
    for t in range(r):
        blk = x_ref[0, :, :, t].reshape(2 * r * r, lanes)
        y = jnp.dot(wa_ref[t], blk, preferred_element_type=F32)
        y_ref[:, :, t] = y.astype(BF16).reshape(2, r, r, lanes)
    for rr in range(r):
        blk = y_ref[:, rr].reshape(2 * r * r, lanes)
        z = jnp.dot(wb_ref[rr], blk, preferred_element_type=F32)
        o_ref[0, :, rr] = z.astype(BF16).reshape(2, r, r, lanes)


def _fourier_ab(xcs, stage_a, stage_b, batch):
    r = DFT_RADIX
    c = FOURIER_WIDTH
    nn = 2 * r * r
    lanes = DFT_SLAB_LANES
    slab = pl.BlockSpec((1, 2, r, r, r, lanes), lambda b, j: (b, 0, 0, 0, 0, j))
    return pl.pallas_call(
        _dft_ab_kernel,
        grid=(batch, c // lanes),
        in_specs=[slab, _const_spec((r, nn, nn)), _const_spec((r, nn, nn))],
        out_specs=slab,
        out_shape=jax.ShapeDtypeStruct((batch, 2, r, r, r, c), BF16),
        scratch_shapes=[pltpu.VMEM((2, r, r, r, lanes), BF16)],
        compiler_params=pltpu.CompilerParams(
            dimension_semantics=("arbitrary", "arbitrary"), vmem_limit_bytes=VMEM_LIMIT),
        name="seq_dft_stages_ab",
    )(xcs.reshape(batch, 2, r, r, r, c), stage_a, stage_b)


def _bias_tiles_kernel(tab_ref, o_ref):
    h = pl.program_id(0)
    tt = ATTN_TILE
    half = REL_BUCKETS // 2
    max_exact = half // 2
    o_ref[0, 0] = jnp.full((tt, tt), tab_ref[half - 1, h] * LOG2E, F32)
    o_ref[0, 4] = jnp.full((tt, tt), tab_ref[REL_BUCKETS - 1, h] * LOG2E, F32)
    wide = 2 * tt
    lane = lax.broadcasted_iota(jnp.int32, (8, wide), 1)
    diff = jnp.where(lane < tt, -lane, wide - lane)
    for t in range(1, 4):
        rel = diff + (t - 2) * tt
        ret = jnp.where(rel > 0, half, 0)
        n = jnp.abs(rel)
        nf = jnp.maximum(n, 1).astype(F32)
        large = max_exact + (jnp.log(nf / max_exact) / math.log(REL_MAX_DIST / max_exact)
                             * (half - max_exact)).astype(jnp.int32)
        large = jnp.minimum(large, half - 1)
        bucket = ret + jnp.where(n < max_exact, n, large)
        lo = half if t > 2 else 0
        hi = half if t < 2 else REL_BUCKETS
        acc = jnp.zeros((8, wide), F32)
        for bk in range(lo, hi):
            acc = jnp.where(bucket == bk, tab_ref[bk, h], acc)
        rows = jnp.broadcast_to(acc[0:1] * LOG2E, (tt, wide))
        o_ref[0, t] = pltpu.roll(rows, 0, 1, stride=1, stride_axis=0)[:, :tt]


def _bias_tiles(table):
    tt = ATTN_TILE
    return pl.pallas_call(
        _bias_tiles_kernel,
        grid=(DIFF_HEADS,),
        in_specs=[pl.BlockSpec(memory_space=pltpu.SMEM)],
        out_specs=pl.BlockSpec((1, 5, tt, tt), lambda h: (h, 0, 0, 0)),
        out_shape=jax.ShapeDtypeStruct((DIFF_HEADS, 5, tt, tt), F32),
        compiler_params=pltpu.CompilerParams(
            dimension_semantics=("arbitrary",), vmem_limit_bytes=VMEM_LIMIT),
        name="rel_bias_tiles",
    )(table)


def _attn_kernel(tab_ref, q_ref, k_ref, vt_ref, bias_ref, lam_ref, g_ref, o_ref, *,
                 lam_init, n_kv):
    h = pl.program_id(0)
    tt = ATTN_TILE
    n_sub = q_ref.shape[0] // tt
    c_after = tab_ref[REL_BUCKETS - 1, h] * LOG2E
    c_before = tab_ref[REL_BUCKETS // 2 - 1, h] * LOG2E
    lf = lam_ref[...]
    lam = (jnp.exp(jnp.sum(lf[0:1] * lf[1:2], axis=-1, keepdims=True))
           - jnp.exp(jnp.sum(lf[2:3] * lf[3:4], axis=-1, keepdims=True)) + lam_init)

    tiles = [pl.program_id(2) * n_sub + sub for sub in range(n_sub)]
    qm = []
    for sub in range(n_sub):
        qt = q_ref[sub * tt:(sub + 1) * tt, :].astype(F32).T
        row = lax.broadcasted_iota(jnp.int32, qt.shape, 0)
        qm.append([jnp.where(row < DIFF_QK_DIM, qt, 0.0).astype(BF16),
                   jnp.where(row >= DIFF_QK_DIM, qt, 0.0).astype(BF16)])

    def scores(sub, d, mp):
        i = tiles[sub]
        j = lax.rem(i + d, n_kv)
        kt = k_ref[pl.ds(pl.multiple_of(j * tt, tt), tt), :]
        s = jnp.dot(kt, qm[sub][mp], preferred_element_type=F32)
        if 2 <= d <= n_kv - 2:
            return s, jnp.where(i + d < n_kv, c_after, c_before)
        return s + bias_ref[0, jnp.clip(j - i, -2, 2) + 2], None

    def finish(sub, acc):
        num = [a[:DIFF_V_DIM] / a[DIFF_V_DIM:DIFF_V_DIM + 1] for a in acc]
        o = num[0] - lam * num[1]
        y = (o * lax.rsqrt(jnp.mean(o * o, axis=0, keepdims=True) + RMS_EPS) * g_ref[...]
             * (1.0 - lam_init))
        o_ref[sub * tt:(sub + 1) * tt, :] = y.T.astype(BF16)

    units = [(sub, d, mp) for sub in range(n_sub) for d in range(n_kv) for mp in range(2)]
    ahead = [scores(*units[v]) for v in range(ATTN_LOOKAHEAD)]
    m = acc = None
    for u, (sub, d, mp) in enumerate(units):
        if d == 0 and mp == 0:
            m, acc = [None, None], [None, None]
        s, const = ahead.pop(0)
        if u + ATTN_LOOKAHEAD < len(units):
            ahead.append(scores(*units[u + ATTN_LOOKAHEAD]))
        vt = vt_ref[0, 0, lax.rem(tiles[sub] + d, n_kv)]
        smax = jnp.max(s, axis=0, keepdims=True)
        if const is not None:
            smax = smax + const
        m_new = smax if m[mp] is None else jnp.maximum(m[mp], smax)
        shift = m_new if const is None else m_new - const
        pv = jnp.dot(vt, jnp.exp2(s - shift).astype(BF16), preferred_element_type=F32)
        acc[mp] = pv if m[mp] is None else jnp.exp2(m[mp] - m_new) * acc[mp] + pv
        m[mp] = m_new
        if d == n_kv - 1 and mp == 1:
            finish(sub, acc)


def _attention(q, k, vt, bias, table, lam, subln_g, lam_init, batch, seq):
    tt = ATTN_TILE
    nq = seq // tt
    tq = ATTN_QTILES * tt
    nsteps = seq // tq
    m = batch * seq
    kern = functools.partial(_attn_kernel, lam_init=lam_init, n_kv=nq)
    return pl.pallas_call(
        kern,
        grid=(DIFF_HEADS, batch, nsteps),
        in_specs=[
            pl.BlockSpec(memory_space=pltpu.SMEM),
            pl.BlockSpec((tq, DIFF_V_DIM), lambda h, b, i: (b * nsteps + i, h)),
            pl.BlockSpec((seq, DIFF_V_DIM), lambda h, b, i: (b, h)),
            pl.BlockSpec((1, 1, nq, VT_ROWS, tt), lambda h, b, i: (b, h, 0, 0, 0)),
            pl.BlockSpec((1, 5, tt, tt), lambda h, b, i: (h, 0, 0, 0)),
            _const_spec((4, DIFF_QK_DIM)),
            _const_spec((DIFF_V_DIM, 1)),
        ],
        out_specs=pl.BlockSpec((tq, DIFF_V_DIM), lambda h, b, i: (b * nsteps + i, h)),
        out_shape=jax.ShapeDtypeStruct((m, DIFF_WIDTH), BF16),
        compiler_params=pltpu.CompilerParams(
            dimension_semantics=("arbitrary", "arbitrary", "arbitrary"),
            vmem_limit_bytes=VMEM_LIMIT),
        name="diff_attention",
    )(table, q, k, vt, bias, lam, subln_g)


def _swiglu_residual(x, g_ref, w1_ref, w3_ref, w2_ref):
    hn = _rms(x, g_ref[...]).astype(BF16)
    acc = x
    for c in range(D_FF // FF_CHUNK):
        cols = slice(c * FF_CHUNK, (c + 1) * FF_CHUNK)
        a = jnp.dot(hn, w1_ref[:, cols], preferred_element_type=F32)
        b = jnp.dot(hn, w3_ref[:, cols], preferred_element_type=F32)
        u = (a * jax.nn.sigmoid(a) * b).astype(BF16)
        acc = acc + jnp.dot(u, w2_ref[cols, :], preferred_element_type=F32)
    return acc


def _ffn_kernel(x_ref, g_ref, w1_ref, w3_ref, w2_ref, gf_ref, o_ref, *, final_norm):
    acc = _swiglu_residual(x_ref[...], g_ref, w1_ref, w3_ref, w2_ref)
    if final_norm:
        acc = _rms(acc, gf_ref[...])
    o_ref[...] = acc


def _ffn(x, g, w1, w3, w2, gf, final_norm):
    m = x.shape[0]
    tm = ROW_TILE
    return pl.pallas_call(
        functools.partial(_ffn_kernel, final_norm=final_norm),
        grid=(m // tm,),
        in_specs=[
            pl.BlockSpec((tm, D_MODEL), lambda i: (i, 0)),
            _const_spec((1, D_MODEL)),
            _const_spec((D_MODEL, D_FF)),
            _const_spec((D_MODEL, D_FF)),
            _const_spec((D_FF, D_MODEL)),
            _const_spec((1, D_MODEL)),
        ],
        out_specs=pl.BlockSpec((tm, D_MODEL), lambda i: (i, 0)),
        out_shape=jax.ShapeDtypeStruct((m, D_MODEL), F32),
        compiler_params=pltpu.CompilerParams(
            dimension_semantics=("arbitrary",), vmem_limit_bytes=VMEM_LIMIT),
        name="swiglu_ffn",
    )(x, g, w1, w3, w2, gf)


def _even_tail_kernel(z_ref, x_ref, a_ref, sc_ref, wo_ref, g_ref, w1_ref, w3_ref, w2_ref,
                      o_ref):
    nb = z_ref.shape[0]
    rows = nb * DFT_RADIX * DFT_RADIX
    f = jnp.concatenate(
        [jnp.dot(sc_ref[...], z_ref[b].reshape(sc_ref.shape[1], FOURIER_WIDTH),
                 preferred_element_type=F32).astype(BF16) for b in range(nb)], axis=0)
    x = x_ref[...].reshape(rows, D_MODEL)
    mix = jnp.concatenate([f, a_ref[...].reshape(rows, DIFF_WIDTH)], axis=1)
    h = x + jnp.dot(mix, wo_ref[...], preferred_element_type=F32)
    o_ref[...] = _swiglu_residual(h, g_ref, w1_ref, w3_ref, w2_ref).reshape(o_ref.shape)


def _even_tail(z, x, att, stage_c, w_out, g, w1, w3, w2, batch):
    r = DFT_RADIX
    nb = EVEN_TAIL_BATCHES
    c = FOURIER_WIDTH

    def tiles(width):
        return pl.BlockSpec((nb, r, None, r, width), lambda d, b: (b, 0, d, 0, 0))

    out = pl.pallas_call(
        _even_tail_kernel,
        grid=(r, batch // nb),
        in_specs=[
            pl.BlockSpec((nb, 2, r, None, r, c), lambda d, b: (b, 0, 0, d, 0, 0)),
            tiles(D_MODEL),
            tiles(DIFF_WIDTH),
            _const_spec((r * r, 2 * r * r)),
            _const_spec((FOURIER_WIDTH + DIFF_WIDTH, D_MODEL)),
            _const_spec((1, D_MODEL)),
            _const_spec((D_MODEL, D_FF)),
            _const_spec((D_MODEL, D_FF)),
            _const_spec((D_FF, D_MODEL)),
        ],
        out_specs=tiles(D_MODEL),
        out_shape=jax.ShapeDtypeStruct((batch, r, r, r, D_MODEL), F32),
        compiler_params=pltpu.CompilerParams(
            dimension_semantics=("arbitrary", "arbitrary"), vmem_limit_bytes=VMEM_LIMIT),
        name="even_tail",
    )(z, x.reshape(batch, r, r, r, D_MODEL), att.reshape(batch, r, r, r, DIFF_WIDTH),
      stage_c, w_out, g, w1, w3, w2)
    return out.reshape(batch * SEQ_LEN, D_MODEL)


def _odd_kernel(x_ref, g_ref, wuv_ref, gv_ref, ws_ref, bs_ref, wo_ref, o_ref, y_ref):
    tm = ROW_TILE
    subs = [slice(t * tm, (t + 1) * tm) for t in range(x_ref.shape[0] // tm)]
    zs = [jnp.dot(_rms(x_ref[rows, :], g_ref[...]).astype(BF16), wuv_ref[...],
                  preferred_element_type=F32) for rows in subs]
    gd = SGU_WIDTH // SGU_GROUPS
    for rows, z in zip(subs, zs):
        z = 0.5 * z * (1.0 + lax.erf(z * math.sqrt(0.5)))
        u = z[:, :SGU_WIDTH]
        vn = _rms(z[:, SGU_WIDTH:], gv_ref[...]).astype(BF16)
        for r in range(tm // SGU_CHUNK):
            chunk = slice(r * SGU_CHUNK, (r + 1) * SGU_CHUNK)
            out_rows = slice(rows.start + chunk.start, rows.start + chunk.stop)
            for g in range(SGU_GROUPS):
                cols = slice(g * gd, (g + 1) * gd)
                sv = jnp.dot(ws_ref[g], vn[chunk, cols], preferred_element_type=F32) + bs_ref[g]
                y_ref[out_rows, cols] = (u[chunk, cols] * sv).astype(BF16)
        o_ref[rows, :] = x_ref[rows, :] + jnp.dot(y_ref[rows, :], wo_ref[...],
                                                  preferred_element_type=F32)


def _odd_mixer(x, g, wuv, gv, ws, bs, wo):
    m = x.shape[0]
    tm = SGU_SUBTILES * ROW_TILE
    gd = SGU_WIDTH // SGU_GROUPS
    return pl.pallas_call(
        _odd_kernel,
        grid=(m // tm,),
        in_specs=[
            pl.BlockSpec((tm, D_MODEL), lambda i: (i, 0)),
            _const_spec((1, D_MODEL)),
            _const_spec((D_MODEL, 2 * SGU_WIDTH)),
            _const_spec((1, SGU_WIDTH)),
            _const_spec((SGU_GROUPS, SGU_CHUNK, SGU_CHUNK)),
            _const_spec((SGU_GROUPS, SGU_CHUNK, gd)),
            _const_spec((SGU_WIDTH, D_MODEL)),
        ],
        out_specs=pl.BlockSpec((tm, D_MODEL), lambda i: (i, 0)),
        out_shape=jax.ShapeDtypeStruct((m, D_MODEL), F32),
        scratch_shapes=[pltpu.VMEM((tm, SGU_WIDTH), BF16)],
        compiler_params=pltpu.CompilerParams(
            dimension_semantics=("arbitrary",), vmem_limit_bytes=VMEM_LIMIT),
        name="sgu_mixer",
    )(x, g, wuv, gv, ws, bs, wo)


def _cast_kernel(*refs):
    n = len(refs) // 2
    for src, dst in zip(refs[:n], refs[n:]):
        dst[...] = src[...].astype(BF16)


def _cast_weights(items):
    steps = CAST_STEPS
    in_specs, out_specs, out_shape = [], [], []
    for arr, layer in items:
        _, rows, cols = arr.shape
        blk = rows // steps
        assert blk * steps == rows and blk % 16 == 0
        in_specs.append(pl.BlockSpec((None, blk, cols), lambda s, layer=layer: (layer, s, 0)))
        out_specs.append(pl.BlockSpec((blk, cols), lambda s: (s, 0)))
        out_shape.append(jax.ShapeDtypeStruct((rows, cols), BF16))
    return pl.pallas_call(
        _cast_kernel, grid=(steps,), in_specs=in_specs, out_specs=out_specs,
        out_shape=out_shape,
        compiler_params=pltpu.CompilerParams(
            dimension_semantics=("arbitrary",), vmem_limit_bytes=VMEM_LIMIT),
        name="cast_weights",
    )(*[arr for arr, _ in items])


def kernel(x, rel_bias_table, norm_mix_g, norm_ffn_g, even_w_in, even_w_out, diff_lambda,
           diff_subln_g, odd_w_uv, odd_v_norm_g, odd_w_s, odd_b_s, odd_w_out, ffn_w1, ffn_w3,
           ffn_w2, final_norm_g):
    batch, seq, d = x.shape
    assert (seq, d) == (SEQ_LEN, D_MODEL)
    depth = norm_mix_g.shape[0]
    assert depth % 2 == 0
    m = batch * seq
    h = x.reshape(m, d)
    stage_a, stage_b, stage_c, cdft = _dft_constants()

    items, slot = [], {}
    def want(name, arr, layer):
        slot[name, layer] = len(items)
        items.append((arr.reshape(arr.shape[0], -1, arr.shape[-1]), layer))
    for i in range(depth):
        jx = i // 2
        if i % 2 == 0:
            want('w_in', even_w_in, jx)
            want('w_out_even', even_w_out, jx)
        else:
            want('w_uv', odd_w_uv, jx)
            want('w_s', odd_w_s, jx)
            want('w_out_odd', odd_w_out, jx)
        want('w1', ffn_w1, i)
        want('w3', ffn_w3, i)
        want('w2', ffn_w2, i)
    cast = _cast_weights(items)
    w = lambda name, layer: cast[slot[name, layer]]

    for i in range(depth):
        jx = i // 2
        g_mix = norm_mix_g[i].reshape(1, d)
        g_ffn = norm_ffn_g[i].reshape(1, d)
        if i % 2 == 0:
            lam_init = 0.8 - 0.6 * math.exp(-0.3 * i)
            wv_t = even_w_in[jx][:, FOURIER_WIDTH + 2 * DIFF_WIDTH:].T.astype(BF16)
            xcs, q, k, vt = _even_in(h, g_mix, w('w_in', jx), wv_t, cdft, batch, seq)
            z = _fourier_ab(xcs, stage_a, stage_b, batch)
            bias = _bias_tiles(rel_bias_table)
            att = _attention(q, k, vt, bias, rel_bias_table, diff_lambda[jx],
                             diff_subln_g[jx].reshape(DIFF_V_DIM, 1), lam_init, batch, seq)
            h = _even_tail(z, h, att, stage_c, w('w_out_even', jx), g_ffn,
                           w('w1', i), w('w3', i), w('w2', i), batch)
        else:
            bs = jnp.broadcast_to(odd_b_s[jx][:, :, None],
                                  (SGU_GROUPS, SGU_CHUNK, SGU_WIDTH // SGU_GROUPS))
            h = _odd_mixer(h, g_mix, w('w_uv', jx), odd_v_norm_g[jx].reshape(1, SGU_WIDTH),
                           w('w_s', jx).reshape(SGU_GROUPS, SGU_CHUNK, SGU_CHUNK), bs,
                           w('w_out_odd', jx))
            h = _ffn(h, g_ffn, w('w1', i), w('w3', i), w('w2', i),
                     final_norm_g.reshape(1, d), i == depth - 1)
    return h.reshape(batch, seq, d)
```
